```python
import math
import jax
import jax.numpy as jnp
from jax import lax
import numpy as np


D_MODEL = 1024
BATCH = 8
SEQ = 4096
DEPTH = 4

HEAD_DIM = 64
NSA_HEADS = D_MODEL // (2 * HEAD_DIM)
NSA_KV_HEADS = 2
NSA_GROUP = NSA_HEADS // NSA_KV_HEADS
CMP_LEN = 32
CMP_STRIDE = 16
CMP_HIDDEN = 128
SEL_BLOCK = 64
N_SEL = 16
WINDOW = 512
Q_BLOCK = 128
FORCE_BONUS = 1.0e4
ROPE_THETA = 500000.0
ROPE_DIM = HEAD_DIM // 4

GDN_HEADS = D_MODEL // (4 * HEAD_DIM)
GDN_CONV = 4
GDN_CHUNK = 64

CONF_CH = D_MODEL // 4
CONF_WIDTH = 31

N_GROUPS = 4
EXPERTS_PER_GROUP = 8
N_EXPERTS = N_GROUPS * EXPERTS_PER_GROUP
TOP_K_IN_GROUP = 2
EXPERT_FF = 512
MOE_BLOCK = 256

NSA_WIDTH = NSA_HEADS * HEAD_DIM
KV_WIDTH = NSA_KV_HEADS * HEAD_DIM
GDN_WIDTH = GDN_HEADS * HEAD_DIM
MIX_WIDTH = NSA_WIDTH + GDN_WIDTH + CONF_CH
SPLIT_SIZES = (NSA_WIDTH, 6 * KV_WIDTH, 3 * NSA_HEADS, 3 * GDN_WIDTH, GDN_WIDTH, GDN_HEADS, GDN_HEADS, 2 * CONF_CH)
D_IN = NSA_WIDTH + 6 * KV_WIDTH + 3 * NSA_HEADS + 4 * GDN_WIDTH + 2 * GDN_HEADS + 2 * CONF_CH

DEEPNORM_ALPHA = (2.0 * DEPTH) ** 0.25
DEEPNORM_BETA = (8.0 * DEPTH) ** -0.25

kernel_name = 'hybrid_nsa_gdn_conformer_hmoe'


def _layer_norm(x, w, b, eps=1e-5):
    xf = x.astype(jnp.float32)
    mu = jnp.mean(xf, axis=-1, keepdims=True)
    var = jnp.mean(jnp.square(xf - mu), axis=-1, keepdims=True)
    return ((xf - mu) * lax.rsqrt(var + eps) * w + b).astype(x.dtype)


def _l2norm(x, eps=1e-6):
    xf = x.astype(jnp.float32)
    return xf * lax.rsqrt(jnp.sum(xf * xf, axis=-1, keepdims=True) + eps)


def _rope_tables(seq):
    inv = ROPE_THETA ** (-jnp.arange(0, ROPE_DIM, 2, dtype=jnp.float32) / ROPE_DIM)
    ang = jnp.arange(seq, dtype=jnp.float32)[:, None] * inv[None, :]
    return jnp.cos(ang), jnp.sin(ang)


def _partial_rope(x, cos, sin):
    half = ROPE_DIM // 2
    x1 = x[..., :half].astype(jnp.float32)
    x2 = x[..., half:ROPE_DIM].astype(jnp.float32)
    c = cos[:, None, :]
    s = sin[:, None, :]
    rot = jnp.concatenate([x1 * c - x2 * s, x2 * c + x1 * s], axis=-1).astype(x.dtype)
    return jnp.concatenate([rot, x[..., ROPE_DIM:]], axis=-1)


def _causal_depthwise_conv(x, w):
    k = w.shape[0]
    return lax.conv_general_dilated(
        x, w[:, None, :].astype(x.dtype), window_strides=(1,), padding=((k - 1, 0),),
        dimension_numbers=('NWC', 'WIO', 'NWC'), feature_group_count=x.shape[-1])


def _masked_softmax(s, mask):
    s = jnp.where(mask, s.astype(jnp.float32), -jnp.inf)
    m = jnp.max(s, axis=-1, keepdims=True)
    m = jnp.where(jnp.isfinite(m), m, 0.0)
    e = jnp.exp(s - m)
    return e / jnp.maximum(jnp.sum(e, axis=-1, keepdims=True), 1e-30)


def _compress(t, pe, w1, w2):
    B, S = t.shape[:2]
    n_chunk = S // CMP_STRIDE
    r = CMP_LEN // CMP_STRIDE
    n_cmp = n_chunk - r + 1
    ch = t.reshape(B, n_chunk, CMP_STRIDE, NSA_KV_HEADS, HEAD_DIM)
    blk = jnp.concatenate([ch[:, i:i + n_cmp] for i in range(r)], axis=2)
    blk = blk + pe[None, None, :, None, :]
    blk = blk.transpose(0, 1, 3, 2, 4).reshape(B, n_cmp, NSA_KV_HEADS, CMP_LEN * HEAD_DIM)
    return jax.nn.silu(blk @ w1) @ w2


def _overlap_matrix(n_cmp, n_sb):
    c0 = jnp.arange(n_cmp)[:, None] * CMP_STRIDE
    s0 = jnp.arange(n_sb)[None, :] * SEL_BLOCK
    ov = jnp.minimum(c0 + CMP_LEN, s0 + SEL_BLOCK) - jnp.maximum(c0, s0)
    return (jnp.maximum(ov, 0) / CMP_STRIDE).astype(jnp.float32)


def _nsa(q, k_cmp, v_cmp, k_sel, v_sel, k_win, v_win, gates, cmp_pe, cmp_w1, cmp_w2):
    B, S = q.shape[:2]
    n_sb = S // SEL_BLOCK
    n_sel = min(N_SEL, n_sb)
    n_qb = S // Q_BLOCK
    kc = _compress(k_cmp, cmp_pe[0], cmp_w1[0], cmp_w2[0])
    vc = _compress(v_cmp, cmp_pe[1], cmp_w1[1], cmp_w2[1])
    n_cmp = kc.shape[1]
    cmp_end = jnp.arange(n_cmp) * CMP_STRIDE + (CMP_LEN - 1)
    overlap = _overlap_matrix(n_cmp, n_sb)
    ks_blk = k_sel.reshape(B, n_sb, SEL_BLOCK, NSA_KV_HEADS, HEAD_DIM).transpose(0, 3, 1, 2, 4)
    vs_blk = v_sel.reshape(B, n_sb, SEL_BLOCK, NSA_KV_HEADS, HEAD_DIM).transpose(0, 3, 1, 2, 4)
    pad = ((0, 0), (WINDOW, 0), (0, 0), (0, 0))
    kw_pad = jnp.pad(k_win, pad)
    vw_pad = jnp.pad(v_win, pad)
    b_ix = jnp.arange(B)[:, None, None, None]
    h_ix = jnp.arange(NSA_KV_HEADS)[None, :, None, None]
    blk_ids = jnp.arange(n_sb)
    scale = HEAD_DIM ** -0.5

    def query_block(args):
        qb, gb, s0 = args
        t = s0 + jnp.arange(Q_BLOCK)
        qg = qb.reshape(B, Q_BLOCK, NSA_KV_HEADS, NSA_GROUP, HEAD_DIM) * scale
        sc = jnp.einsum('bqhgd,bchd->bhgqc', qg, kc)
        pc = _masked_softmax(sc, cmp_end[None, :] <= t[:, None])
        o_cmp = jnp.einsum('bhgqc,bchd->bqhgd', pc.astype(vc.dtype), vc)
        imp = jnp.einsum('bhgqc,cj->bhqj', pc, overlap)
        cur = t // SEL_BLOCK
        forced = (blk_ids[None, :] == 0) | (blk_ids[None, :] == cur[:, None]) | (blk_ids[None, :] == cur[:, None] - 1)
        causal_blk = blk_ids[None, :] <= cur[:, None]
        imp = jnp.where(causal_blk, imp + jnp.where(forced, FORCE_BONUS, 0.0), -jnp.inf)
        _, idx = lax.top_k(imp, n_sel)
        valid = idx <= cur[None, None, :, None]
        kg = ks_blk[b_ix, h_ix, idx]
        vg = vs_blk[b_ix, h_ix, idx]
        ss = jnp.einsum('bqhgd,bhqnkd->bhgqnk', qg, kg)
        tok = idx[..., None] * SEL_BLOCK + jnp.arange(SEL_BLOCK)
        smask = valid[..., None] & (tok <= t[None, None, :, None, None])
        ps = _masked_softmax(ss.reshape(B, NSA_KV_HEADS, NSA_GROUP, Q_BLOCK, n_sel * SEL_BLOCK),
                             smask.reshape(B, NSA_KV_HEADS, 1, Q_BLOCK, n_sel * SEL_BLOCK))
        o_sel = jnp.einsum('bhgqnk,bhqnkd->bqhgd', ps.reshape(ss.shape).astype(vg.dtype), vg)
        kw = lax.dynamic_slice_in_dim(kw_pad, s0, Q_BLOCK + WINDOW, axis=1)
        vw = lax.dynamic_slice_in_dim(vw_pad, s0, Q_BLOCK + WINDOW, axis=1)
        p = s0 - WINDOW + jnp.arange(Q_BLOCK + WINDOW)
        wmask = (p[None, :] <= t[:, None]) & (p[None, :] > t[:, None] - WINDOW) & (p[None, :] >= 0)
        sw = jnp.einsum('bqhgd,bkhd->bhgqk', qg, kw)
        pw = _masked_softmax(sw, wmask)
        o_win = jnp.einsum('bhgqk,bkhd->bqhgd', pw.astype(vw.dtype), vw)
        o = jnp.stack([o_cmp, o_sel, o_win], axis=-1).reshape(B, Q_BLOCK, NSA_HEADS, HEAD_DIM, 3)
        return jnp.einsum('bqhdr,bqhr->bqhd', o, gb.astype(o.dtype))

    q_blocks = q.reshape(B, n_qb, Q_BLOCK, NSA_HEADS, HEAD_DIM).swapaxes(0, 1)
    g_blocks = gates.reshape(B, n_qb, Q_BLOCK, NSA_HEADS, 3).swapaxes(0, 1)
    starts = jnp.arange(n_qb) * Q_BLOCK
    out = lax.map(query_block, (q_blocks, g_blocks, starts))
    return out.swapaxes(0, 1).reshape(B, S, NSA_WIDTH).astype(q.dtype)


def _chunked_gated_delta_rule(q, k, v, g, beta):
    B, S, H, Dk = q.shape
    C = GDN_CHUNK
    N = S // C

    def chunks(t):
        t = t.astype(jnp.float32)
        return jnp.moveaxis(t.reshape((B, N, C, H) + t.shape[3:]), 3, 1)

    q, k, v, g, beta = chunks(q), chunks(k), chunks(v), chunks(g), chunks(beta)
    q = q * Dk ** -0.5
    g = jnp.cumsum(g, axis=-1)
    tri = jnp.tril(jnp.ones((C, C), dtype=bool))
    tri_strict = jnp.tril(jnp.ones((C, C), dtype=bool), -1)
    decay = jnp.exp(jnp.where(tri, g[..., :, None] - g[..., None, :], -jnp.inf))
    kb = k * beta[..., None]
    a = jnp.where(tri_strict, jnp.einsum('bhnid,bhnjd->bhnij', kb, k) * decay, 0.0)
    t_mat = jnp.eye(C, dtype=jnp.float32) + a
    u = lax.linalg.triangular_solve(t_mat, v * beta[..., None], left_side=True, lower=True)
    w = lax.linalg.triangular_solve(t_mat, kb * jnp.exp(g)[..., None], left_side=True, lower=True)
    qk = jnp.einsum('bhnid,bhnjd->bhnij', q, k) * decay
    q_dec = q * jnp.exp(g)[..., None]
    g_last = g[..., -1]
    k_dec = k * jnp.exp(g_last[..., None] - g)[..., None]

    def step(state, xs):
        qk_n, q_n, k_n, u_n, w_n, gl_n = xs
        v_new = u_n - jnp.einsum('bhcd,bhde->bhce', w_n, state)
        o = jnp.einsum('bhcd,bhde->bhce', q_n, state) + jnp.einsum('bhij,bhje->bhie', qk_n, v_new)
        state = state * jnp.exp(gl_n)[..., None, None] + jnp.einsum('bhcd,bhce->bhde', k_n, v_new)
        return state, o

    xs = tuple(jnp.moveaxis(t, 2, 0) for t in (qk, q_dec, k_dec, u, w, g_last))
    s0 = jnp.zeros((B, H, Dk, v.shape[-1]), jnp.float32)
    _, o = lax.scan(step, s0, xs)
    return o.transpose(1, 0, 3, 2, 4).reshape(B, S, H, v.shape[-1])


def _gated_deltanet(qkv, z, b, a, conv_w, a_log, dt_bias, norm_w):
    B, S, _ = qkv.shape
    h = jax.nn.silu(_causal_depthwise_conv(qkv, conv_w)).reshape(B, S, 3, GDN_HEADS, HEAD_DIM)
    q = _l2norm(h[:, :, 0])
    k = _l2norm(h[:, :, 1])
    v = h[:, :, 2].astype(jnp.float32)
    beta = jax.nn.sigmoid(b.astype(jnp.float32))
    g = -jnp.exp(a_log.astype(jnp.float32)) * jax.nn.softplus(a.astype(jnp.float32) + dt_bias.astype(jnp.float32))
    o = _chunked_gated_delta_rule(q, k, v, g, beta)
    o = o * lax.rsqrt(jnp.mean(o * o, axis=-1, keepdims=True) + 1e-6) * norm_w.astype(jnp.float32)
    o = o * jax.nn.silu(z.astype(jnp.float32).reshape(B, S, GDN_HEADS, HEAD_DIM))
    return o.reshape(B, S, GDN_WIDTH).astype(qkv.dtype)


def _conformer_conv(h, dw_w, dw_b, ln_w, ln_b):
    val, gate = jnp.split(h, 2, axis=-1)
    u = val * jax.nn.sigmoid(gate)
    u = _causal_depthwise_conv(u, dw_w) + dw_b
    return jax.nn.silu(_layer_norm(u, ln_w, ln_b))


def _hier_moe(x, w_group, b_group, w_expert, b_expert, w_gate, w_up, w_down):
    B, S, D = x.shape
    T = B * S
    K = TOP_K_IN_GROUP
    xt = x.reshape(T, D)
    g_prob = jax.nn.softmax((xt @ w_group + b_group).astype(jnp.float32), axis=-1)
    g_w, g_idx = lax.top_k(g_prob, 1)
    e_logits = (xt @ w_expert + b_expert).astype(jnp.float32).reshape(T, N_GROUPS, EXPERTS_PER_GROUP)
    e_in = jnp.take_along_axis(e_logits, g_idx[:, :, None], axis=1)[:, 0]
    e_val, e_loc = lax.top_k(e_in, K)
    e_w = jax.nn.softmax(e_val, axis=-1) * g_w
    e_id = g_idx * EXPERTS_PER_GROUP + e_loc
    TK = T * K
    flat_id = e_id.reshape(TK)
    order = jnp.argsort(flat_id)
    sorted_id = flat_id[order]
    counts = jnp.bincount(flat_id, length=N_EXPERTS)
    padded = (counts + MOE_BLOCK - 1) // MOE_BLOCK * MOE_BLOCK
    seg_start = jnp.cumsum(counts) - counts
    pad_end = jnp.cumsum(padded)
    pad_start = pad_end - padded
    dest = pad_start[sorted_id] + jnp.arange(TK) - seg_start[sorted_id]
    n_blocks = -(-TK // MOE_BLOCK) + N_EXPERTS
    src_tok = order // K
    buf = jnp.zeros((n_blocks * MOE_BLOCK, D), x.dtype).at[dest].set(xt[src_tok])
    blk_expert = jnp.minimum(jnp.searchsorted(pad_end, jnp.arange(n_blocks) * MOE_BLOCK, side='right'), N_EXPERTS - 1)

    def expert_block(args):
        xb, e = args
        hid = jax.nn.silu(xb @ w_gate[e]) * (xb @ w_up[e])
        return hid @ w_down[e]

    y_buf = lax.map(expert_block, (buf.reshape(n_blocks, MOE_BLOCK, D), blk_expert)).reshape(-1, D)
    y = y_buf[dest] * e_w.reshape(TK)[order][:, None].astype(x.dtype)
    return jnp.zeros((T, D), x.dtype).at[src_tok].add(y).reshape(B, S, D)


def setup_inputs(seed: int = 0) -> dict:
    key = jax.random.key(seed)
    ks = jax.random.split(key, 26)
    L = DEPTH

    def nrm(k, shape, scale):
        return jax.random.normal(k, shape, jnp.float32) * scale

    dt = jnp.exp(jax.random.uniform(ks[8], (L, GDN_HEADS), jnp.float32, math.log(1e-3), math.log(1e-1)))
    return {
        'x': nrm(ks[0], (BATCH, SEQ, D_MODEL), 1.0),
        'w_in': nrm(ks[1], (L, D_MODEL, D_IN), D_MODEL ** -0.5),
        'w_out': nrm(ks[2], (L, MIX_WIDTH, D_MODEL), MIX_WIDTH ** -0.5 * DEEPNORM_BETA),
        'nsa_cmp_pe': nrm(ks[3], (L, 2, CMP_LEN, HEAD_DIM), 0.1),
        'nsa_cmp_w1': nrm(ks[4], (L, 2, CMP_LEN * HEAD_DIM, CMP_HIDDEN), (CMP_LEN * HEAD_DIM) ** -0.5),
        'nsa_cmp_w2': nrm(ks[5], (L, 2, CMP_HIDDEN, HEAD_DIM), CMP_HIDDEN ** -0.5),
        'gdn_conv_w': nrm(ks[6], (L, GDN_CONV, 3 * GDN_WIDTH), GDN_CONV ** -0.5),
        'gdn_a_log': jnp.log(jax.random.uniform(ks[7], (L, GDN_HEADS), jnp.float32, 1.0, 16.0)),
        'gdn_dt_bias': dt + jnp.log(-jnp.expm1(-dt)),
        'gdn_norm_w': 1.0 + nrm(ks[9], (L, HEAD_DIM), 0.02),
        'conf_dw_w': nrm(ks[10], (L, CONF_WIDTH, CONF_CH), CONF_WIDTH ** -0.5),
        'conf_dw_b': nrm(ks[11], (L, CONF_CH), 0.02),
        'conf_ln_w': 1.0 + nrm(ks[12], (L, CONF_CH), 0.02),
        'conf_ln_b': nrm(ks[13], (L, CONF_CH), 0.02),
        'ln1_w': 1.0 + nrm(ks[14], (L, D_MODEL), 0.02),
        'ln1_b': nrm(ks[15], (L, D_MODEL), 0.02),
        'ln2_w': 1.0 + nrm(ks[16], (L, D_MODEL), 0.02),
        'ln2_b': nrm(ks[17], (L, D_MODEL), 0.02),
        'moe_w_group': nrm(ks[18], (L, D_MODEL, N_GROUPS), D_MODEL ** -0.5),
        'moe_b_group': nrm(ks[19], (L, N_GROUPS), 0.01),
        'moe_w_expert': nrm(ks[20], (L, D_MODEL, N_EXPERTS), D_MODEL ** -0.5),
        'moe_b_expert': nrm(ks[21], (L, N_EXPERTS), 0.01),
        'moe_w_gate': nrm(ks[22], (L, N_EXPERTS, D_MODEL, EXPERT_FF), D_MODEL ** -0.5),
        'moe_w_up': nrm(ks[23], (L, N_EXPERTS, D_MODEL, EXPERT_FF), D_MODEL ** -0.5),
        'moe_w_down': nrm(ks[24], (L, N_EXPERTS, EXPERT_FF, D_MODEL), EXPERT_FF ** -0.5 * DEEPNORM_BETA),
    }


def reference(x, w_in, w_out, nsa_cmp_pe, nsa_cmp_w1, nsa_cmp_w2, gdn_conv_w, gdn_a_log, gdn_dt_bias,
              gdn_norm_w, conf_dw_w, conf_dw_b, conf_ln_w, conf_ln_b, ln1_w, ln1_b, ln2_w, ln2_b,
              moe_w_group, moe_b_group, moe_w_expert, moe_b_expert, moe_w_gate, moe_w_up, moe_w_down):
    B, S, _ = x.shape
    cos, sin = _rope_tables(S)
    split_points = [int(p) for p in np.cumsum(SPLIT_SIZES)[:-1]]
    for l in range(DEPTH):
        h = x @ w_in[l]
        q, kv, gate_logits, gdn_qkv, gdn_z, gdn_b, gdn_a, conf_in = jnp.split(h, split_points, axis=-1)
        q = _partial_rope(q.reshape(B, S, NSA_HEADS, HEAD_DIM), cos, sin)
        kv = kv.reshape(B, S, 6, NSA_KV_HEADS, HEAD_DIM)
        k_cmp = _partial_rope(kv[:, :, 0], cos, sin)
        k_sel = _partial_rope(kv[:, :, 2], cos, sin)
        k_win = _partial_rope(kv[:, :, 4], cos, sin)
        gates = jax.nn.sigmoid(gate_logits).reshape(B, S, NSA_HEADS, 3)
        y_nsa = _nsa(q, k_cmp, kv[:, :, 1], k_sel, kv[:, :, 3], k_win, kv[:, :, 5], gates,
                     nsa_cmp_pe[l], nsa_cmp_w1[l], nsa_cmp_w2[l])
        y_gdn = _gated_deltanet(gdn_qkv, gdn_z, gdn_b, gdn_a, gdn_conv_w[l], gdn_a_log[l], gdn_dt_bias[l], gdn_norm_w[l])
        y_conf = _conformer_conv(conf_in, conf_dw_w[l], conf_dw_b[l], conf_ln_w[l], conf_ln_b[l])
        mix = jnp.concatenate([y_nsa, y_gdn, y_conf], axis=-1) @ w_out[l]
        x = _layer_norm(DEEPNORM_ALPHA * x + mix, ln1_w[l], ln1_b[l])
        moe = _hier_moe(x, moe_w_group[l], moe_b_group[l], moe_w_expert[l], moe_b_expert[l],
                        moe_w_gate[l], moe_w_up[l], moe_w_down[l])
        x = _layer_norm(DEEPNORM_ALPHA * x + moe, ln2_w[l], ln2_b[l])
    return x
```

```python
import functools

import jax
import jax.numpy as jnp
import numpy as np
from jax import lax
from jax.experimental import pallas as pl
from jax.experimental.pallas import tpu as pltpu

F32 = jnp.float32
BF16 = jnp.bfloat16
I32 = jnp.int32

D_MODEL = 1024
DEPTH = 4
HEAD_DIM = 64
NSA_HEADS = 8
NSA_KV_HEADS = 2
NSA_GROUP = NSA_HEADS // NSA_KV_HEADS
CMP_LEN = 32
CMP_STRIDE = 16
CMP_HIDDEN = 128
SEL_BLOCK = 64
N_SEL = 16
WINDOW = 512
Q_BLOCK = 128
FORCE_BONUS = 1.0e4
ROPE_THETA = 500000.0
ROPE_DIM = HEAD_DIM // 4
GDN_HEADS = 4
GDN_CONV = 4
GDN_CHUNK = 64
CONF_CH = 256
CONF_WIDTH = 31
N_GROUPS = 4
EXPERTS_PER_GROUP = 8
N_EXPERTS = N_GROUPS * EXPERTS_PER_GROUP
EXPERT_FF = 512
NSA_WIDTH = NSA_HEADS * HEAD_DIM
KV_WIDTH = NSA_KV_HEADS * HEAD_DIM
GDN_WIDTH = GDN_HEADS * HEAD_DIM
DEEPNORM_ALPHA = (2.0 * DEPTH) ** 0.25
LN_EPS = 1e-5

LANES = 128
V7X_VMEM_BYTES = 64 * 1024 * 1024
VMEM_LIMIT = V7X_VMEM_BYTES * 3 // 4

Q_OFF = 0
KV_OFF = Q_OFF + NSA_WIDTH
GQKV_OFF = KV_OFF + 6 * KV_WIDTH
GZ_OFF = GQKV_OFF + 3 * GDN_WIDTH
CONF_OFF = GZ_OFF + GDN_WIDTH
SMALL_OFF = CONF_OFF + 2 * CONF_CH
IN_COLS = SMALL_OFF + LANES
N_SMALL = 3 * NSA_HEADS + 2 * GDN_HEADS

ROW_TILE = 512
MOE_ROWS = 256
SEL_KEYS = 512
NEG = -1e30


def _params(*sem):
    return pltpu.CompilerParams(dimension_semantics=sem, vmem_limit_bytes=VMEM_LIMIT)


def _dot(a, b):
    return jnp.dot(a, b, preferred_element_type=F32)


def _dot_nt(a, b):
    return lax.dot_general(a, b, (((1,), (1,)), ((), ())), preferred_element_type=F32)


def _bmm(a, b):
    return lax.dot_general(a.astype(BF16), b.astype(BF16), (((2,), (1,)), ((0,), (0,))),
                           preferred_element_type=F32)


def _bmm_nt(a, b):
    return lax.dot_general(a.astype(BF16), b.astype(BF16), (((2,), (2,)), ((0,), (0,))),
                           preferred_element_type=F32)


def _sigmoid(x):
    return 1.0 / (1.0 + jnp.exp(-x))


def _silu(x):
    return x * _sigmoid(x)


def _layer_norm(v, w, b):
    mu = jnp.mean(v, axis=-1, keepdims=True)
    d = v - mu
    var = jnp.mean(d * d, axis=-1, keepdims=True)
    return d * lax.rsqrt(var + LN_EPS) * w + b


def _inproj_kernel(x_ref, w_ref, cos_ref, sa_ref, sb_ref, q_ref, kv_ref, gqkv_ref, gz_ref, conf_ref, small_ref):
    xb = x_ref[...].astype(BF16)
    cos = cos_ref[...]
    sa = sa_ref[...]
    sb = sb_ref[...]

    def rope(h):
        return h * cos + pltpu.roll(h, LANES - ROPE_DIM // 2, 1) * sa + pltpu.roll(h, ROPE_DIM // 2, 1) * sb

    hq = _dot(xb, w_ref[:, Q_OFF:KV_OFF])
    scale = HEAD_DIM ** -0.5
    for j in range(NSA_WIDTH // LANES):
        q_ref[:, j * LANES:(j + 1) * LANES] = (rope(hq[:, j * LANES:(j + 1) * LANES]) * scale).astype(BF16)
    hkv = _dot(xb, w_ref[:, KV_OFF:GQKV_OFF])
    for j in range(6):
        h = hkv[:, j * LANES:(j + 1) * LANES]
        if j % 2 == 0:
            h = rope(h)
        kv_ref[:, j * LANES:(j + 1) * LANES] = h.astype(BF16)
    gqkv_ref[...] = _dot(xb, w_ref[:, GQKV_OFF:GZ_OFF])
    gz_ref[...] = _dot(xb, w_ref[:, GZ_OFF:CONF_OFF])
    conf_ref[...] = _dot(xb, w_ref[:, CONF_OFF:SMALL_OFF])
    small_ref[...] = _dot(xb, w_ref[:, SMALL_OFF:IN_COLS])


def _inproj(x2, w, cos, sa, sb, seq):
    t = x2.shape[0]
    tm = min(ROW_TILE, seq)
    n_pos = seq // tm
    row = lambda i: (i, 0)
    pos = lambda i: (i % n_pos, 0)
    widths = (NSA_WIDTH, 6 * KV_WIDTH, 3 * GDN_WIDTH, GDN_WIDTH, 2 * CONF_CH, LANES)
    dtypes = (BF16, BF16, F32, F32, F32, F32)
    return pl.pallas_call(
        _inproj_kernel,
        grid=(t // tm,),
        in_specs=[pl.BlockSpec((tm, D_MODEL), row),
                  pl.BlockSpec((D_MODEL, IN_COLS), lambda i: (0, 0)),
                  pl.BlockSpec((tm, LANES), pos), pl.BlockSpec((tm, LANES), pos), pl.BlockSpec((tm, LANES), pos)],
        out_specs=[pl.BlockSpec((tm, wd), row) for wd in widths],
        out_shape=[jax.ShapeDtypeStruct((t, wd), dt) for wd, dt in zip(widths, dtypes)],
        compiler_params=_params("parallel"),
        name="inproj",
    )(x2, w, cos, sa, sb)


def _rope_tables(seq):
    half = ROPE_DIM // 2
    inv = ROPE_THETA ** (-jnp.arange(0, ROPE_DIM, 2, dtype=F32) / ROPE_DIM)
    ang = jnp.arange(seq, dtype=F32)[:, None] * inv[None, :]
    c, s = jnp.cos(ang), jnp.sin(ang)
    ones = jnp.ones((seq, HEAD_DIM - ROPE_DIM), F32)
    zeros = jnp.zeros((seq, HEAD_DIM - ROPE_DIM), F32)
    zh = jnp.zeros((seq, half), F32)
    cos_h = jnp.concatenate([c, c, ones], axis=1)
    sa_h = jnp.concatenate([-s, zh, zeros], axis=1)
    sb_h = jnp.concatenate([zh, s, zeros], axis=1)
    rep = LANES // HEAD_DIM
    return jnp.tile(cos_h, (1, rep)), jnp.tile(sa_h, (1, rep)), jnp.tile(sb_h, (1, rep))


def _permute_w_in(w):
    o_q, o_kv = 0, NSA_WIDTH
    o_gate = o_kv + 6 * KV_WIDTH
    o_gqkv = o_gate + 3 * NSA_HEADS
    o_gz = o_gqkv + 3 * GDN_WIDTH
    o_b = o_gz + GDN_WIDTH
    o_a = o_b + GDN_HEADS
    o_conf = o_a + GDN_HEADS
    pad = jnp.zeros((w.shape[0], LANES - N_SMALL), w.dtype)
    return jnp.concatenate([w[:, o_q:o_gate], w[:, o_gqkv:o_b], w[:, o_conf:], w[:, o_gate:o_gqkv],
                            w[:, o_b:o_conf], pad], axis=1).astype(BF16)


def _compress_kernel(ch_ref, pet_ref, peb_ref, w1t_ref, w1b_ref, w2_ref, o_ref):
    ch = ch_ref[0, 0].astype(F32)
    top = _dot((ch + pet_ref[0]).astype(BF16), w1t_ref[0])
    bot = _dot((ch + peb_ref[0]).astype(BF16), w1b_ref[0])
    n = bot.shape[0]
    hid = top + pltpu.roll(bot, n - 1, 0)
    o_ref[0, 0] = _dot(_silu(hid).astype(BF16), w2_ref[0]).astype(o_ref.dtype)


def _compress(chunks, pe_top, pe_bot, w1_top, w1_bot, w2):
    _, bh, n_chunk, feat = chunks.shape
    per = lambda r, i: (r, 0, 0)
    return pl.pallas_call(
        _compress_kernel,
        grid=(2, bh),
        in_specs=[pl.BlockSpec((1, 1, n_chunk, feat), lambda r, i: (r, i, 0, 0)),
                  pl.BlockSpec((1, 1, feat), per), pl.BlockSpec((1, 1, feat), per),
                  pl.BlockSpec((1, feat, CMP_HIDDEN), per), pl.BlockSpec((1, feat, CMP_HIDDEN), per),
                  pl.BlockSpec((1, CMP_HIDDEN, HEAD_DIM), per)],
        out_specs=pl.BlockSpec((1, 1, n_chunk, HEAD_DIM), lambda r, i: (r, i, 0, 0)),
        out_shape=jax.ShapeDtypeStruct((2, bh, n_chunk, HEAD_DIM), BF16),
        compiler_params=_params("parallel", "parallel"),
        name="nsa_compress",
    )(chunks, pe_top, pe_bot, w1_top, w1_bot, w2)


def _nsa_kernel(q_ref, kc_ref, vc_ref, ks_ref, vs_ref, kw_ref, vw_ref, gl_ref, ovt_ref, o_ref, *, seq, n_sel):
    g, qn = NSA_GROUP, Q_BLOCK
    rows = g * qn
    n_c = seq // CMP_STRIDE
    n_sb = seq // SEL_BLOCK
    s0 = pl.program_id(2) * qn
    q = q_ref[0].reshape(rows, HEAD_DIM)

    sc = _dot_nt(q, kc_ref[0, 0, 0])
    tq = s0 + lax.broadcasted_iota(I32, (g, qn, n_c), 1)
    cend = lax.broadcasted_iota(I32, (g, qn, n_c), 2) * CMP_STRIDE + (CMP_LEN - 1)
    sc = jnp.where(cend <= tq, sc.reshape(g, qn, n_c), -jnp.inf)
    m = jnp.max(sc, axis=-1, keepdims=True)
    m = jnp.where(m == -jnp.inf, 0.0, m)
    e = jnp.exp(sc - m)
    pc = e / jnp.maximum(jnp.sum(e, axis=-1, keepdims=True), 1e-30)
    o_cmp = _dot(pc.reshape(rows, n_c).astype(BF16), vc_ref[0, 0, 0])

    pcs = jnp.sum(pc, axis=0)
    imp = lax.dot_general(ovt_ref[...], pcs, (((1,), (1,)), ((), ())), preferred_element_type=F32,
                          precision=lax.Precision.HIGHEST)
    blk = lax.broadcasted_iota(I32, (n_sb, qn), 0)
    cur = (s0 + lax.broadcasted_iota(I32, (n_sb, qn), 1)) // SEL_BLOCK
    forced = (blk == 0) | (blk == cur) | (blk == cur - 1)
    causal = blk <= cur
    imp = jnp.where(causal, imp + jnp.where(forced, FORCE_BONUS, 0.0), -jnp.inf)
    rank = jnp.zeros((n_sb, qn), F32)
    for i in range(n_sb):
        vi = imp[i:i + 1, :]
        ahead = (vi > imp) | ((vi == imp) & (blk > i))
        rank = rank + jnp.where(ahead, 1.0, 0.0)
    sel_t = jnp.where((rank < n_sel) & causal, 1.0, 0.0)
    sel = sel_t.T.astype(BF16)

    kstep = min(SEL_KEYS, seq)
    n_steps = (s0 + qn + kstep - 1) // kstep

    def sel_step(c, carry):
        m_i, l_i, acc = carry
        k0 = pl.multiple_of(c * kstep, kstep)
        s = _dot_nt(q, ks_ref[0, 0, 0, pl.ds(k0, kstep), :]).reshape(g, qn, kstep)
        key_blk = (k0 + lax.broadcasted_iota(I32, (n_sb, kstep), 1)) // SEL_BLOCK
        expand = jnp.where(key_blk == lax.broadcasted_iota(I32, (n_sb, kstep), 0), 1.0, 0.0).astype(BF16)
        picked = _dot(sel, expand)
        key = k0 + lax.broadcasted_iota(I32, (qn, kstep), 1)
        ok = (picked > 0.5) & (key <= s0 + lax.broadcasted_iota(I32, (qn, kstep), 0))
        s = jnp.where(ok[None], s, NEG)
        m_new = jnp.maximum(m_i, jnp.max(s, axis=-1, keepdims=True))
        alpha = jnp.exp(m_i - m_new)
        p = jnp.exp(s - m_new)
        l_new = alpha * l_i + jnp.sum(p, axis=-1, keepdims=True)
        pv = _dot(p.reshape(rows, kstep).astype(BF16), vs_ref[0, 0, 0, pl.ds(k0, kstep), :])
        return m_new, l_new, alpha * acc + pv.reshape(g, qn, HEAD_DIM)

    init = (jnp.full((g, qn, 1), NEG, F32), jnp.zeros((g, qn, 1), F32), jnp.zeros((g, qn, HEAD_DIM), F32))
    _, l_s, acc_s = lax.fori_loop(0, n_steps, sel_step, init)
    o_sel = (acc_s / l_s).reshape(rows, HEAD_DIM)

    wlen = WINDOW + qn
    w0 = pl.multiple_of(jnp.maximum(s0 - WINDOW, 0), qn)
    sw = _dot_nt(q, kw_ref[0, 0, 0, pl.ds(w0, wlen), :]).reshape(g, qn, wlen)
    pos = w0 + lax.broadcasted_iota(I32, (qn, wlen), 1)
    tw = s0 + lax.broadcasted_iota(I32, (qn, wlen), 0)
    okw = (pos <= tw) & (pos > tw - WINDOW)
    sw = jnp.where(okw[None], sw, NEG)
    pw = jnp.exp(sw - jnp.max(sw, axis=-1, keepdims=True))
    pw = jnp.where(okw[None], pw, 0.0)
    lw = jnp.sum(pw, axis=-1, keepdims=True)
    o_win = _dot(pw.reshape(rows, wlen).astype(BF16), vw_ref[0, 0, 0, pl.ds(w0, wlen), :])
    o_win = o_win / lw.reshape(rows, 1)

    gate = _sigmoid(gl_ref[0, 0].reshape(rows, 3))
    o = o_cmp * gate[:, 0:1] + o_sel * gate[:, 1:2] + o_win * gate[:, 2:3]
    o_ref[0] = o.reshape(g, qn, HEAD_DIM).astype(o_ref.dtype)


def _overlap_t(n_c, n_sb):
    c0 = jnp.arange(n_c)[None, :] * CMP_STRIDE
    b0 = jnp.arange(n_sb)[:, None] * SEL_BLOCK
    ov = jnp.minimum(c0 + CMP_LEN, b0 + SEL_BLOCK) - jnp.maximum(c0, b0)
    return (jnp.maximum(ov, 0) / CMP_STRIDE).astype(F32)


def _nsa_attention(q4, kcvc, kv6, gates5, ov_t, seq):
    b = q4.shape[0]
    n_c = seq // CMP_STRIDE
    n_sb = seq // SEL_BLOCK
    n_sel = min(N_SEL, n_sb)

    def kv_spec(r, n):
        return pl.BlockSpec((1, 1, 1, n, HEAD_DIM), lambda bi, h, i: (r, bi, h, 0, 0))

    return pl.pallas_call(
        functools.partial(_nsa_kernel, seq=seq, n_sel=n_sel),
        grid=(b, NSA_KV_HEADS, seq // Q_BLOCK),
        in_specs=[pl.BlockSpec((1, NSA_GROUP, Q_BLOCK, HEAD_DIM), lambda bi, h, i: (bi, h, i, 0)),
                  kv_spec(0, n_c), kv_spec(1, n_c),
                  kv_spec(2, seq), kv_spec(3, seq), kv_spec(4, seq), kv_spec(5, seq),
                  pl.BlockSpec((1, 1, NSA_GROUP, Q_BLOCK, 3), lambda bi, h, i: (bi, h, 0, i, 0)),
                  pl.BlockSpec((n_sb, n_c), lambda bi, h, i: (0, 0))],
        out_specs=pl.BlockSpec((1, NSA_GROUP, Q_BLOCK, HEAD_DIM), lambda bi, h, i: (bi, h, i, 0)),
        out_shape=jax.ShapeDtypeStruct((b, NSA_HEADS, seq, HEAD_DIM), BF16),
        compiler_params=_params("parallel", "parallel", "arbitrary"),
        name="nsa_attention",
    )(q4, kcvc, kcvc, kv6, kv6, kv6, kv6, gates5, ov_t)


def _gdn_conv_kernel(cur_ref, prev_ref, w_ref, o_ref, ext_ref):
    ts = cur_ref.shape[1]
    halo = prev_ref.shape[1]
    first = pl.program_id(1) == 0
    ext_ref[0:halo, :] = jnp.where(first, 0.0, prev_ref[0])
    ext_ref[halo:halo + ts, :] = cur_ref[0]
    acc = cur_ref[0] * w_ref[GDN_CONV - 1:GDN_CONV, :]
    for k in range(GDN_CONV - 1):
        acc = acc + ext_ref[pl.ds(halo - (GDN_CONV - 1) + k, ts), :] * w_ref[k:k + 1, :]
    o_ref[0] = _silu(acc)


def _gdn_conv(x3, w):
    b, seq, ch = x3.shape
    ts = min(ROW_TILE, seq)
    halo = 8
    per_tile = ts // halo
    return pl.pallas_call(
        _gdn_conv_kernel,
        grid=(b, seq // ts),
        in_specs=[pl.BlockSpec((1, ts, ch), lambda bi, i: (bi, i, 0)),
                  pl.BlockSpec((1, halo, ch), lambda bi, i: (bi, jnp.maximum(i * per_tile - 1, 0), 0)),
                  pl.BlockSpec((GDN_CONV, ch), lambda bi, i: (0, 0))],
        out_specs=pl.BlockSpec((1, ts, ch), lambda bi, i: (bi, i, 0)),
        out_shape=jax.ShapeDtypeStruct((b, seq, ch), F32),
        scratch_shapes=[pltpu.VMEM((ts + halo, ch), F32)],
        compiler_params=_params("parallel", "parallel"),
        name="gdn_conv",
    )(x3, x3, w)


def _l2norm(v):
    return v * lax.rsqrt(jnp.sum(v * v, axis=-1, keepdims=True) + 1e-6)


def _softplus(v):
    return jnp.maximum(v, 0.0) + jnp.log1p(jnp.exp(-jnp.abs(v)))


def _gdn_kernel(q_ref, k_ref, v_ref, z_ref, b_ref, ac_ref, ar_ref, alog_ref, dtb_ref, nw_ref, o_ref, state_ref,
                *, chunks):
    c_len = GDN_CHUNK

    @pl.when(pl.program_id(1) == 0)
    def _():
        state_ref[...] = jnp.zeros_like(state_ref)

    ri = lax.broadcasted_iota(I32, (c_len, c_len), 0)
    ci = lax.broadcasted_iota(I32, (c_len, c_len), 1)
    lower = (ri >= ci)[None]
    strict = (ri > ci)[None]
    upper = (ri <= ci)[None]
    eye = jnp.where(ri == ci, 1.0, 0.0)[None]
    rate = -jnp.exp(alog_ref[...])
    dtb = dtb_ref[...]
    nw = nw_ref[...]

    def chunk(c, carry):
        r0 = pl.multiple_of(c * c_len, c_len)
        rows = pl.ds(r0, c_len)
        q = _l2norm(q_ref[0, 0, :, rows, :]) * HEAD_DIM ** -0.5
        k = _l2norm(k_ref[0, 0, :, rows, :])
        v = v_ref[0, 0, :, rows, :]
        beta = _sigmoid(b_ref[0, :, rows, :])
        g_col = rate * _softplus(ac_ref[0, :, rows, :] + dtb)
        g_row = rate * _softplus(ar_ref[0, :, c] + dtb)
        gc_col = jnp.sum(jnp.where(lower, g_row, 0.0), axis=2, keepdims=True)
        gc_row = jnp.sum(jnp.where(upper, g_col, 0.0), axis=1, keepdims=True)
        decay = jnp.exp(jnp.where(lower, gc_col - gc_row, -jnp.inf))
        kb = k * beta
        a = jnp.where(strict, _bmm_nt(kb, k) * decay, 0.0)
        inv = eye - a
        pw = a
        for _ in range(int(np.log2(c_len)) - 1):
            pw = _bmm(pw, pw)
            inv = inv + _bmm(inv, pw)
        e_col = jnp.exp(gc_col)
        u = _bmm(inv, v * beta)
        w = _bmm(inv, kb * e_col)
        qk = _bmm_nt(q, k) * decay
        g_last = gc_col[:, c_len - 1:c_len, :]
        k_dec = k * jnp.exp(g_last - gc_col)
        state = state_ref[...]
        v_new = u - _bmm(w, state)
        o = _bmm(q * e_col, state) + _bmm(qk, v_new)
        state_ref[...] = state * jnp.exp(g_last) + _bmm(jnp.swapaxes(k_dec, 1, 2), v_new)
        o = o * lax.rsqrt(jnp.mean(o * o, axis=-1, keepdims=True) + 1e-6) * nw
        o_ref[0, :, rows, :] = (o * _silu(z_ref[0, :, rows, :])).astype(o_ref.dtype)
        return carry

    lax.fori_loop(0, chunks, chunk, 0)


def _gdn(qkvh, zh, b_col, a_col, a_row, a_log, dt_bias, norm_w):
    _, b, h, seq, dh = qkvh.shape
    chunks = min(8, seq // GDN_CHUNK)
    rows = chunks * GDN_CHUNK

    def qkv_spec(r):
        return pl.BlockSpec((1, 1, h, rows, dh), lambda bi, i: (r, bi, 0, i, 0))

    col = pl.BlockSpec((1, h, rows, 1), lambda bi, i: (bi, 0, i, 0))
    head = pl.BlockSpec((h, 1, 1), lambda bi, i: (0, 0, 0))
    return pl.pallas_call(
        functools.partial(_gdn_kernel, chunks=chunks),
        grid=(b, seq // rows),
        in_specs=[qkv_spec(0), qkv_spec(1), qkv_spec(2),
                  pl.BlockSpec((1, h, rows, dh), lambda bi, i: (bi, 0, i, 0)),
                  col, col,
                  pl.BlockSpec((1, h, chunks, 1, GDN_CHUNK), lambda bi, i: (bi, 0, i, 0, 0)),
                  head, head,
                  pl.BlockSpec((1, dh), lambda bi, i: (0, 0))],
        out_specs=pl.BlockSpec((1, h, rows, dh), lambda bi, i: (bi, 0, i, 0)),
        out_shape=jax.ShapeDtypeStruct((b, h, seq, dh), BF16),
        scratch_shapes=[pltpu.VMEM((h, dh, dh), F32)],
        compiler_params=_params("parallel", "arbitrary"),
        name="gdn_delta_rule",
    )(qkvh, qkvh, qkvh, zh, b_col, a_col, a_row, a_log, dt_bias, norm_w)


def _conf_kernel(cur_ref, prev_ref, w_ref, b_ref, lnw_ref, lnb_ref, o_ref, ext_ref):
    ts = cur_ref.shape[1]
    halo = prev_ref.shape[1]
    first = pl.program_id(1) == 0

    def glu(blk):
        return blk[:, :CONF_CH] * _sigmoid(blk[:, CONF_CH:])

    ext_ref[0:halo, :] = jnp.where(first, 0.0, glu(prev_ref[0]))
    ext_ref[halo:halo + ts, :] = glu(cur_ref[0])
    acc = jnp.zeros((ts, CONF_CH), F32) + b_ref[...]
    for k in range(CONF_WIDTH):
        acc = acc + ext_ref[pl.ds(halo - (CONF_WIDTH - 1) + k, ts), :] * w_ref[k:k + 1, :]
    o_ref[0] = _silu(_layer_norm(acc, lnw_ref[...], lnb_ref[...])).astype(o_ref.dtype)


def _conformer(x3, w, bias, ln_w, ln_b):
    b, seq, ch = x3.shape
    ts = min(ROW_TILE, seq)
    halo = 32
    per_tile = ts // halo
    vec = pl.BlockSpec((1, CONF_CH), lambda bi, i: (0, 0))
    return pl.pallas_call(
        _conf_kernel,
        grid=(b, seq // ts),
        in_specs=[pl.BlockSpec((1, ts, ch), lambda bi, i: (bi, i, 0)),
                  pl.BlockSpec((1, halo, ch), lambda bi, i: (bi, jnp.maximum(i * per_tile - 1, 0), 0)),
                  pl.BlockSpec((CONF_WIDTH, CONF_CH), lambda bi, i: (0, 0)), vec, vec, vec],
        out_specs=pl.BlockSpec((1, ts, CONF_CH), lambda bi, i: (bi, i, 0)),
        out_shape=jax.ShapeDtypeStruct((b, seq, CONF_CH), BF16),
        scratch_shapes=[pltpu.VMEM((ts + halo, CONF_CH), F32)],
        compiler_params=_params("parallel", "parallel"),
        name="conformer_conv",
    )(x3, x3, w, bias, ln_w, ln_b)


def _outproj_kernel(yn_ref, yg_ref, yc_ref, x_ref, w_ref, lnw_ref, lnb_ref, wr_ref, br_ref, x1_ref, x1b_ref, lg_ref):
    mix = _dot(yn_ref[...], w_ref[0:NSA_WIDTH, :])
    mix = mix + _dot(yg_ref[...], w_ref[NSA_WIDTH:NSA_WIDTH + GDN_WIDTH, :])
    mix = mix + _dot(yc_ref[...], w_ref[NSA_WIDTH + GDN_WIDTH:, :])
    x1 = _layer_norm(DEEPNORM_ALPHA * x_ref[...] + mix, lnw_ref[...], lnb_ref[...])
    x1_ref[...] = x1
    xb = x1.astype(BF16)
    x1b_ref[...] = xb
    lg_ref[...] = _dot(xb, wr_ref[...]) + br_ref[...]


def _outproj(y_nsa, y_gdn, y_conf, x2, w, ln_w, ln_b, wr, br):
    t = x2.shape[0]
    tm = min(ROW_TILE, t)
    row = lambda i: (i, 0)
    fixed = lambda i: (0, 0)
    return pl.pallas_call(
        _outproj_kernel,
        grid=(t // tm,),
        in_specs=[pl.BlockSpec((tm, NSA_WIDTH), row), pl.BlockSpec((tm, GDN_WIDTH), row),
                  pl.BlockSpec((tm, CONF_CH), row), pl.BlockSpec((tm, D_MODEL), row),
                  pl.BlockSpec((D_MODEL, D_MODEL), fixed), pl.BlockSpec((1, D_MODEL), fixed),
                  pl.BlockSpec((1, D_MODEL), fixed), pl.BlockSpec((D_MODEL, LANES), fixed),
                  pl.BlockSpec((1, LANES), fixed)],
        out_specs=[pl.BlockSpec((tm, D_MODEL), row), pl.BlockSpec((tm, D_MODEL), row), pl.BlockSpec((tm, LANES), row)],
        out_shape=[jax.ShapeDtypeStruct((t, D_MODEL), F32), jax.ShapeDtypeStruct((t, D_MODEL), BF16),
                   jax.ShapeDtypeStruct((t, LANES), F32)],
        compiler_params=_params("parallel"),
        name="outproj_ln_router",
    )(y_nsa, y_gdn, y_conf, x2, w, ln_w, ln_b, wr, br)


def _route_kernel(lg_ref, route_ref, counts_ref, carry_ref):
    tm = lg_ref.shape[0]

    @pl.when(pl.program_id(0) == 0)
    def _():
        carry_ref[...] = jnp.zeros_like(carry_ref)

    lg = lg_ref[...]
    lane = lax.broadcasted_iota(I32, (tm, LANES), 1)
    lane_f = lane.astype(F32)

    def top1(vals):
        best = jnp.max(vals, axis=1, keepdims=True)
        idx = jnp.min(jnp.where(vals == best, lane_f, float(LANES)), axis=1, keepdims=True)
        return best, idx

    gl = jnp.where(lane < N_GROUPS, lg, -jnp.inf)
    g_best, g_idx = top1(gl)
    g_w = 1.0 / jnp.sum(jnp.exp(gl - g_best), axis=1, keepdims=True)
    lo = N_GROUPS + g_idx * EXPERTS_PER_GROUP
    el = jnp.where((lane_f >= lo) & (lane_f < lo + EXPERTS_PER_GROUP), lg, -jnp.inf)
    e1, i1 = top1(el)
    e2, i2 = top1(jnp.where(lane_f == i1, -jnp.inf, el))
    r = jnp.exp(e2 - e1)
    w1 = g_w / (1.0 + r)
    w2 = g_w * r / (1.0 + r)
    hit1 = lane_f == i1
    hit2 = lane_f == i2
    onehot = jnp.where(hit1 | hit2, 1.0, 0.0)
    ri = lax.broadcasted_iota(I32, (tm, tm), 0)
    ci = lax.broadcasted_iota(I32, (tm, tm), 1)
    before = jnp.where(ri > ci, 1.0, 0.0).astype(BF16)
    seen = carry_ref[...] + _dot(before, onehot.astype(BF16))
    p1 = jnp.sum(jnp.where(hit1, seen, 0.0), axis=1, keepdims=True)
    p2 = jnp.sum(jnp.where(hit2, seen, 0.0), axis=1, keepdims=True)
    total = carry_ref[...] + jnp.sum(onehot, axis=0, keepdims=True)
    carry_ref[...] = total
    counts_ref[...] = total
    cols = (i1 - N_GROUPS, i2 - N_GROUPS, w1, w2, p1, p2)
    out = jnp.zeros((tm, LANES), F32)
    for j, cval in enumerate(cols):
        out = jnp.where(lane == j, cval, out)
    route_ref[...] = out


def _route(logits):
    t = logits.shape[0]
    tm = min(ROW_TILE, t)
    return pl.pallas_call(
        _route_kernel,
        grid=(t // tm,),
        in_specs=[pl.BlockSpec((tm, LANES), lambda i: (i, 0))],
        out_specs=[pl.BlockSpec((tm, LANES), lambda i: (i, 0)), pl.BlockSpec((1, LANES), lambda i: (0, 0))],
        out_shape=[jax.ShapeDtypeStruct((t, LANES), F32), jax.ShapeDtypeStruct((1, LANES), F32)],
        scratch_shapes=[pltpu.VMEM((1, LANES), F32)],
        compiler_params=_params("arbitrary"),
        name="moe_route",
    )(logits)


def _expert_kernel(be_ref, nu_ref, x_ref, wg_ref, wu_ref, wd_ref, o_ref, wgb_ref, wub_ref, wdb_ref):
    i = pl.program_id(0)
    changed = (i == 0) | (be_ref[i] != be_ref[jnp.maximum(i - 1, 0)])

    @pl.when(changed)
    def _():
        wgb_ref[...] = wg_ref[0, 0].astype(BF16)
        wub_ref[...] = wu_ref[0, 0].astype(BF16)
        wdb_ref[...] = wd_ref[0, 0].astype(BF16)

    @pl.when(i < nu_ref[0])
    def _():
        xb = x_ref[...]
        hid = _silu(_dot(xb, wgb_ref[...])) * _dot(xb, wub_ref[...])
        o_ref[...] = _dot(hid.astype(BF16), wdb_ref[...])

    @pl.when(i >= nu_ref[0])
    def _():
        o_ref[...] = jnp.zeros_like(o_ref)


def _experts(blk_expert, n_used, buf, w_gate, w_up, w_down, layer):
    rows = buf.shape[0]
    n_blocks = rows // MOE_ROWS
    wspec_in = pl.BlockSpec((1, 1, D_MODEL, EXPERT_FF), lambda i, be, nu: (layer, be[i], 0, 0))
    wspec_out = pl.BlockSpec((1, 1, EXPERT_FF, D_MODEL), lambda i, be, nu: (layer, be[i], 0, 0))
    return pl.pallas_call(
        _expert_kernel,
        grid_spec=pltpu.PrefetchScalarGridSpec(
            num_scalar_prefetch=2,
            grid=(n_blocks,),
            in_specs=[pl.BlockSpec((MOE_ROWS, D_MODEL), lambda i, be, nu: (i, 0)), wspec_in, wspec_in, wspec_out],
            out_specs=pl.BlockSpec((MOE_ROWS, D_MODEL), lambda i, be, nu: (i, 0)),
            scratch_shapes=[pltpu.VMEM((D_MODEL, EXPERT_FF), BF16), pltpu.VMEM((D_MODEL, EXPERT_FF), BF16),
                            pltpu.VMEM((EXPERT_FF, D_MODEL), BF16)]),
        out_shape=jax.ShapeDtypeStruct((rows, D_MODEL), F32),
        compiler_params=_params("arbitrary"),
        name="moe_experts",
    )(blk_expert, n_used, buf, w_gate, w_up, w_down)


def _combine_kernel(x1_ref, y0_ref, y1_ref, route_ref, lnw_ref, lnb_ref, o_ref):
    route = route_ref[...]
    moe = y0_ref[...] * route[:, 2:3] + y1_ref[...] * route[:, 3:4]
    o_ref[...] = _layer_norm(DEEPNORM_ALPHA * x1_ref[...] + moe, lnw_ref[...], lnb_ref[...])


def _combine(x1, y0, y1, route, ln_w, ln_b):
    t = x1.shape[0]
    tm = min(ROW_TILE, t)
    row = lambda i: (i, 0)
    fixed = lambda i: (0, 0)
    big = pl.BlockSpec((tm, D_MODEL), row)
    return pl.pallas_call(
        _combine_kernel,
        grid=(t // tm,),
        in_specs=[big, big, big, pl.BlockSpec((tm, LANES), row),
                  pl.BlockSpec((1, D_MODEL), fixed), pl.BlockSpec((1, D_MODEL), fixed)],
        out_specs=big,
        out_shape=jax.ShapeDtypeStruct((t, D_MODEL), F32),
        compiler_params=_params("parallel"),
        name="moe_combine_ln",
    )(x1, y0, y1, route, ln_w, ln_b)


def _mix_heads(x2, l, b, seq, tables, w_in, nsa_cmp_pe, nsa_cmp_w1, nsa_cmp_w2, gdn_conv_w, gdn_a_log, gdn_dt_bias,
               gdn_norm_w, conf_dw_w, conf_dw_b, conf_ln_w, conf_ln_b):
    cos, sa, sb, ov_t = tables
    t = b * seq
    n_c = seq // CMP_STRIDE
    n_gc = seq // GDN_CHUNK
    half = CMP_LEN // 2
    feat = CMP_STRIDE * HEAD_DIM
    q, kv, gqkv, gz, conf_in, small = _inproj(x2, _permute_w_in(w_in[l]), cos, sa, sb, seq)

    q4 = q.reshape(b, seq, NSA_HEADS, HEAD_DIM).transpose(0, 2, 1, 3)
    kv6 = kv.reshape(b, seq, 6, NSA_KV_HEADS, HEAD_DIM).transpose(2, 0, 3, 1, 4)
    chunks = kv6[0:2].reshape(2, b * NSA_KV_HEADS, n_c, feat)
    pe = nsa_cmp_pe[l]
    kcvc = _compress(chunks, pe[:, :half].reshape(2, 1, feat), pe[:, half:].reshape(2, 1, feat),
                     nsa_cmp_w1[l][:, :feat].astype(BF16), nsa_cmp_w1[l][:, feat:].astype(BF16),
                     nsa_cmp_w2[l].astype(BF16))
    kcvc = kcvc.reshape(2, b, NSA_KV_HEADS, n_c, HEAD_DIM)
    gates5 = small[:, :3 * NSA_HEADS].reshape(b, seq, NSA_KV_HEADS, NSA_GROUP, 3).transpose(0, 2, 3, 1, 4)
    y_nsa = _nsa_attention(q4, kcvc, kv6, gates5, ov_t, seq)
    y_nsa = y_nsa.transpose(0, 2, 1, 3).reshape(t, NSA_WIDTH)

    conv = _gdn_conv(gqkv.reshape(b, seq, 3 * GDN_WIDTH), gdn_conv_w[l])
    qkvh = conv.reshape(b, seq, 3, GDN_HEADS, HEAD_DIM).transpose(2, 0, 3, 1, 4)
    zh = gz.reshape(b, seq, GDN_HEADS, HEAD_DIM).transpose(0, 2, 1, 3)
    o_b = 3 * NSA_HEADS
    b_t = small[:, o_b:o_b + GDN_HEADS].reshape(b, seq, GDN_HEADS).transpose(0, 2, 1)
    a_t = small[:, o_b + GDN_HEADS:o_b + 2 * GDN_HEADS].reshape(b, seq, GDN_HEADS).transpose(0, 2, 1)
    y_gdn = _gdn(qkvh, zh, b_t[..., None], a_t[..., None], a_t.reshape(b, GDN_HEADS, n_gc, 1, GDN_CHUNK),
                 gdn_a_log[l].reshape(GDN_HEADS, 1, 1), gdn_dt_bias[l].reshape(GDN_HEADS, 1, 1),
                 gdn_norm_w[l].reshape(1, HEAD_DIM))
    y_gdn = y_gdn.transpose(0, 2, 1, 3).reshape(t, GDN_WIDTH)

    y_conf = _conformer(conf_in.reshape(b, seq, 2 * CONF_CH), conf_dw_w[l], conf_dw_b[l].reshape(1, CONF_CH),
                        conf_ln_w[l].reshape(1, CONF_CH), conf_ln_b[l].reshape(1, CONF_CH)).reshape(t, CONF_CH)
    return y_nsa, y_gdn, y_conf


def _moe_block(x1, x1b, logits, l, ln_w, ln_b, moe_w_gate, moe_w_up, moe_w_down):
    t = x1.shape[0]
    tk = 2 * t
    n_blocks = tk // MOE_ROWS + N_EXPERTS
    route, counts = _route(logits)
    eid = route[:, 0:2].astype(I32)
    pos = route[:, 4:6].astype(I32)
    cnt = counts[0, N_GROUPS:N_GROUPS + N_EXPERTS].astype(I32)
    padded = (cnt + MOE_ROWS - 1) // MOE_ROWS * MOE_ROWS
    pad_end = jnp.cumsum(padded)
    pad_start = pad_end - padded
    dest = pad_start[eid] + pos
    src = jnp.zeros((n_blocks * MOE_ROWS,), I32).at[dest.reshape(tk)].set(jnp.arange(tk, dtype=I32) // 2)
    blk_expert = jnp.minimum(jnp.searchsorted(pad_end, jnp.arange(n_blocks, dtype=I32) * MOE_ROWS, side='right'),
                             N_EXPERTS - 1).astype(I32)
    n_used = (pad_end[-1:] // MOE_ROWS).astype(I32)
    y_buf = _experts(blk_expert, n_used, x1b[src], moe_w_gate, moe_w_up, moe_w_down, l)
    return _combine(x1, y_buf[dest[:, 0]], y_buf[dest[:, 1]], route, ln_w, ln_b)


def kernel(x, w_in, w_out, nsa_cmp_pe, nsa_cmp_w1, nsa_cmp_w2, gdn_conv_w, gdn_a_log, gdn_dt_bias, gdn_norm_w, conf_dw_w, conf_dw_b, conf_ln_w, conf_ln_b, ln1_w, ln1_b, ln2_w, ln2_b, moe_w_group, moe_b_group, moe_w_expert, moe_b_expert, moe_w_gate, moe_w_up, moe_w_down):
    b, seq, d = x.shape
    depth = w_in.shape[0]
    assert d == D_MODEL and seq % ROW_TILE == 0 and seq >= WINDOW + Q_BLOCK
    t = b * seq
    tables = _rope_tables(seq) + (_overlap_t(seq // CMP_STRIDE, seq // SEL_BLOCK),)
    w_out_b = w_out.astype(BF16)
    n_pad = LANES - N_GROUPS - N_EXPERTS
    x2 = x.reshape(t, d)
    for l in range(depth):
        y_nsa, y_gdn, y_conf = _mix_heads(x2, l, b, seq, tables, w_in, nsa_cmp_pe, nsa_cmp_w1, nsa_cmp_w2, gdn_conv_w,
                                          gdn_a_log, gdn_dt_bias, gdn_norm_w, conf_dw_w, conf_dw_b, conf_ln_w, conf_ln_b)
        wr = jnp.concatenate([moe_w_group[l], moe_w_expert[l], jnp.zeros((d, n_pad), F32)], axis=1).astype(BF16)
        br = jnp.concatenate([moe_b_group[l], moe_b_expert[l], jnp.zeros((n_pad,), F32)]).reshape(1, LANES)
        x1, x1b, logits = _outproj(y_nsa, y_gdn, y_conf, x2, w_out_b[l], ln1_w[l].reshape(1, d),
                                   ln1_b[l].reshape(1, d), wr, br)
        x2 = _moe_block(x1, x1b, logits, l, ln2_w[l].reshape(1, d), ln2_b[l].reshape(1, d),
                        moe_w_gate, moe_w_up, moe_w_down)
    return x2.reshape(b, seq, d)
```

```python
import functools

import jax
import jax.numpy as jnp
import numpy as np
from jax import lax
from jax.experimental import pallas as pl
from jax.experimental.pallas import tpu as pltpu

F32 = jnp.float32
BF16 = jnp.bfloat16
I32 = jnp.int32

D_MODEL = 1024
DEPTH = 4
HEAD_DIM = 64
NSA_HEADS = 8
NSA_KV_HEADS = 2
NSA_GROUP = NSA_HEADS // NSA_KV_HEADS
CMP_LEN = 32
CMP_STRIDE = 16
CMP_HIDDEN = 128
SEL_BLOCK = 64
N_SEL = 16
WINDOW = 512
Q_BLOCK = 128
FORCE_BONUS = 1.0e4
ROPE_THETA = 500000.0
ROPE_DIM = HEAD_DIM // 4
GDN_HEADS = 4
GDN_CONV = 4
GDN_CHUNK = 64
CONF_CH = 256
CONF_WIDTH = 31
N_GROUPS = 4
EXPERTS_PER_GROUP = 8
N_EXPERTS = N_GROUPS * EXPERTS_PER_GROUP
EXPERT_FF = 512
NSA_WIDTH = NSA_HEADS * HEAD_DIM
KV_WIDTH = NSA_KV_HEADS * HEAD_DIM
GDN_WIDTH = GDN_HEADS * HEAD_DIM
DEEPNORM_ALPHA = (2.0 * DEPTH) ** 0.25
LN_EPS = 1e-5

LANES = 128
V7X_VMEM_BYTES = 64 * 1024 * 1024
VMEM_LIMIT = V7X_VMEM_BYTES * 3 // 4

Q_OFF = 0
KV_OFF = Q_OFF + NSA_WIDTH
GQKV_OFF = KV_OFF + 6 * KV_WIDTH
GZ_OFF = GQKV_OFF + 3 * GDN_WIDTH
CONF_OFF = GZ_OFF + GDN_WIDTH
SMALL_OFF = CONF_OFF + 2 * CONF_CH
IN_COLS = SMALL_OFF + LANES
N_SMALL = 3 * NSA_HEADS + 2 * GDN_HEADS

ROW_TILE = 512
MOE_ROWS = 256
SEL_KEYS = 512
NEG = -1e30


def _params(*sem):
    return pltpu.CompilerParams(dimension_semantics=sem, vmem_limit_bytes=VMEM_LIMIT)


def _dot(a, b):
    return jnp.dot(a, b, preferred_element_type=F32)


def _dot_nt(a, b):
    return lax.dot_general(a, b, (((1,), (1,)), ((), ())), preferred_element_type=F32)


def _bmm(a, b):
    return lax.dot_general(a.astype(BF16), b.astype(BF16), (((2,), (1,)), ((0,), (0,))),
                           preferred_element_type=F32)


def _bmm_nt(a, b):
    return lax.dot_general(a.astype(BF16), b.astype(BF16), (((2,), (2,)), ((0,), (0,))),
                           preferred_element_type=F32)


def _sigmoid(x):
    return 1.0 / (1.0 + jnp.exp(-x))


def _silu(x):
    return x * _sigmoid(x)


def _layer_norm(v, w, b):
    mu = jnp.mean(v, axis=-1, keepdims=True)
    d = v - mu
    var = jnp.mean(d * d, axis=-1, keepdims=True)
    return d * lax.rsqrt(var + LN_EPS) * w + b


def _inproj_kernel(x_ref, w_ref, cos_ref, sa_ref, sb_ref, oh_ref, q_ref, kvc_ref, ks_ref, kw_ref, vst_ref, vwt_ref,
                   gqkv_ref, gz_ref, conf_ref, small_ref):
    xb = x_ref[...].astype(BF16)
    cos = cos_ref[...]
    sa = sa_ref[...]
    sb = sb_ref[...]
    tm = xb.shape[0]

    def rope(h):
        return h * cos + pltpu.roll(h, LANES - ROPE_DIM // 2, 1) * sa + pltpu.roll(h, ROPE_DIM // 2, 1) * sb

    hq = _dot(xb, w_ref[:, Q_OFF:KV_OFF])
    scale = HEAD_DIM ** -0.5
    for j in range(NSA_WIDTH // LANES):
        q_ref[:, j * LANES:(j + 1) * LANES] = (rope(hq[:, j * LANES:(j + 1) * LANES]) * scale).astype(BF16)
    hkv = _dot(xb, w_ref[:, KV_OFF:GQKV_OFF])
    part = lambda j: hkv[:, j * LANES:(j + 1) * LANES]
    kvc_ref[:, 0:LANES] = rope(part(0)).astype(BF16)
    kvc_ref[:, LANES:2 * LANES] = part(1).astype(BF16)
    ksel = rope(part(2))
    first_half = lax.broadcasted_iota(I32, (tm, LANES), 1) < HEAD_DIM
    oh = oh_ref[...]
    ks_ref[:, 0:LANES] = jnp.where(first_half, ksel, oh).astype(BF16)
    ks_ref[:, LANES:2 * LANES] = jnp.where(first_half, pltpu.roll(ksel, HEAD_DIM, 1), oh).astype(BF16)
    kw_ref[...] = rope(part(4)).astype(BF16)
    vst_ref[0] = part(3).T.astype(BF16)
    vwt = part(5).T.astype(BF16)
    for c in range(tm // Q_BLOCK):
        vwt_ref[c] = vwt[:, c * Q_BLOCK:(c + 1) * Q_BLOCK]
    gqkv_ref[...] = _dot(xb, w_ref[:, GQKV_OFF:GZ_OFF])
    gz_ref[...] = _dot(xb, w_ref[:, GZ_OFF:CONF_OFF])
    conf_ref[...] = _dot(xb, w_ref[:, CONF_OFF:SMALL_OFF])
    small_ref[...] = _dot(xb, w_ref[:, SMALL_OFF:IN_COLS])


def _inproj(x2, w, cos, sa, sb, oh, seq):
    t = x2.shape[0]
    tm = ROW_TILE
    n_pos = seq // tm
    row = lambda i: (i, 0)
    pos = lambda i: (i % n_pos, 0)
    widths = (NSA_WIDTH, 2 * LANES, 2 * LANES, LANES, 3 * GDN_WIDTH, GDN_WIDTH, 2 * CONF_CH, LANES)
    dtypes = (BF16, BF16, BF16, BF16, F32, F32, F32, F32)
    flat_specs = [pl.BlockSpec((tm, wd), row) for wd in widths]
    flat_shapes = [jax.ShapeDtypeStruct((t, wd), dt) for wd, dt in zip(widths, dtypes)]
    n_q = tm // Q_BLOCK
    out_specs = flat_specs[:4] + [pl.BlockSpec((1, LANES, tm), lambda i: (i, 0, 0)),
                                  pl.BlockSpec((n_q, LANES, Q_BLOCK), lambda i: (i, 0, 0))] + flat_specs[4:]
    out_shape = flat_shapes[:4] + [jax.ShapeDtypeStruct((t // tm, LANES, tm), BF16),
                                   jax.ShapeDtypeStruct((t // Q_BLOCK, LANES, Q_BLOCK), BF16)] + flat_shapes[4:]
    return pl.pallas_call(
        _inproj_kernel,
        grid=(t // tm,),
        in_specs=[pl.BlockSpec((tm, D_MODEL), row),
                  pl.BlockSpec((D_MODEL, IN_COLS), lambda i: (0, 0)),
                  pl.BlockSpec((tm, LANES), pos), pl.BlockSpec((tm, LANES), pos), pl.BlockSpec((tm, LANES), pos),
                  pl.BlockSpec((tm, LANES), pos)],
        out_specs=out_specs,
        out_shape=out_shape,
        compiler_params=_params("parallel"),
        name="inproj",
    )(x2, w, cos, sa, sb, oh)


def _block_onehot(seq):
    blk = jnp.arange(seq)[:, None] // SEL_BLOCK
    lane = jnp.arange(LANES)[None, :]
    return jnp.where(lane - HEAD_DIM == blk, 1.0, 0.0).astype(F32)


def _rope_tables(seq):
    half = ROPE_DIM // 2
    inv = ROPE_THETA ** (-jnp.arange(0, ROPE_DIM, 2, dtype=F32) / ROPE_DIM)
    ang = jnp.arange(seq, dtype=F32)[:, None] * inv[None, :]
    c, s = jnp.cos(ang), jnp.sin(ang)
    ones = jnp.ones((seq, HEAD_DIM - ROPE_DIM), F32)
    zeros = jnp.zeros((seq, HEAD_DIM - ROPE_DIM), F32)
    zh = jnp.zeros((seq, half), F32)
    cos_h = jnp.concatenate([c, c, ones], axis=1)
    sa_h = jnp.concatenate([-s, zh, zeros], axis=1)
    sb_h = jnp.concatenate([zh, s, zeros], axis=1)
    rep = LANES // HEAD_DIM
    return jnp.tile(cos_h, (1, rep)), jnp.tile(sa_h, (1, rep)), jnp.tile(sb_h, (1, rep))


def _permute_w_in(w):
    o_q, o_kv = 0, NSA_WIDTH
    o_gate = o_kv + 6 * KV_WIDTH
    o_gqkv = o_gate + 3 * NSA_HEADS
    o_gz = o_gqkv + 3 * GDN_WIDTH
    o_b = o_gz + GDN_WIDTH
    o_a = o_b + GDN_HEADS
    o_conf = o_a + GDN_HEADS
    pad = jnp.zeros((w.shape[0], LANES - N_SMALL), w.dtype)
    return jnp.concatenate([w[:, o_q:o_gate], w[:, o_gqkv:o_b], w[:, o_conf:], w[:, o_gate:o_gqkv],
                            w[:, o_b:o_conf], pad], axis=1).astype(BF16)


def _compress_kernel(ch_ref, pet_ref, peb_ref, w1t_ref, w1b_ref, w2_ref, o_ref, ot_ref):
    ch = ch_ref[0, 0].astype(F32)
    top = _dot((ch + pet_ref[0]).astype(BF16), w1t_ref[0])
    bot = _dot((ch + peb_ref[0]).astype(BF16), w1b_ref[0])
    n = bot.shape[0]
    hid = top + pltpu.roll(bot, n - 1, 0)
    out = _dot(_silu(hid).astype(BF16), w2_ref[0])
    o_ref[0, 0] = out.astype(o_ref.dtype)
    ot_ref[0, 0] = jnp.concatenate([out, jnp.zeros_like(out)], axis=1).T[:HEAD_DIM].astype(ot_ref.dtype)


def _compress(chunks, pe_top, pe_bot, w1_top, w1_bot, w2):
    _, bh, n_chunk, feat = chunks.shape
    per = lambda r, i: (r, 0, 0)
    return pl.pallas_call(
        _compress_kernel,
        grid=(2, bh),
        in_specs=[pl.BlockSpec((1, 1, n_chunk, feat), lambda r, i: (r, i, 0, 0)),
                  pl.BlockSpec((1, 1, feat), per), pl.BlockSpec((1, 1, feat), per),
                  pl.BlockSpec((1, feat, CMP_HIDDEN), per), pl.BlockSpec((1, feat, CMP_HIDDEN), per),
                  pl.BlockSpec((1, CMP_HIDDEN, HEAD_DIM), per)],
        out_specs=[pl.BlockSpec((1, 1, n_chunk, HEAD_DIM), lambda r, i: (r, i, 0, 0)),
                   pl.BlockSpec((1, 1, HEAD_DIM, n_chunk), lambda r, i: (r, i, 0, 0))],
        out_shape=[jax.ShapeDtypeStruct((2, bh, n_chunk, HEAD_DIM), BF16),
                   jax.ShapeDtypeStruct((2, bh, HEAD_DIM, n_chunk), BF16)],
        compiler_params=_params("parallel", "parallel"),
        name="nsa_compress",
    )(chunks, pe_top, pe_bot, w1_top, w1_bot, w2)


def _nsa_kernel(q_ref, kc_ref, vct_ref, ks_ref, vst_ref, kw_ref, vwt_ref, gl_ref, ovt_ref, o_ref, *, seq, n_sel):
    g, qn = NSA_GROUP, Q_BLOCK
    lanes = g * qn
    n_c = seq // CMP_STRIDE
    head = pl.program_id(1)
    qb = pl.program_id(2)
    s0 = qb * qn
    tile = lambda a: jnp.concatenate([a] * g, axis=1)
    qt4 = q_ref[...].astype(F32).T
    q_t = jnp.concatenate([qt4[i * HEAD_DIM:(i + 1) * HEAD_DIM] for i in range(g)], axis=1).astype(BF16)
    t_row = s0 + (lax.broadcasted_iota(I32, (1, lanes), 1) & (qn - 1))
    ones_rows = jnp.ones((16, ROW_TILE), BF16)

    sc = _dot(kc_ref[0, 0], q_t)
    cend = lax.broadcasted_iota(I32, (n_c, lanes), 0) * CMP_STRIDE + (CMP_LEN - 1)
    sc = jnp.where(cend <= t_row, sc, -jnp.inf)
    m = jnp.max(sc, axis=0, keepdims=True)
    m = jnp.where(m == -jnp.inf, 0.0, m)
    e = jnp.exp(sc - m)
    pc = e * (1.0 / jnp.maximum(jnp.sum(e, axis=0, keepdims=True), 1e-30))
    o_cmp = _dot(vct_ref[0, 0], pc.astype(BF16))

    pcs = pc[:, 0:qn]
    for i in range(1, g):
        pcs = pcs + pc[:, i * qn:(i + 1) * qn]
    imp = jnp.dot(ovt_ref[...], pcs, preferred_element_type=F32, precision=lax.Precision.HIGHEST)
    n_blk = imp.shape[0]
    blk = lax.broadcasted_iota(I32, (n_blk, qn), 0)
    cur = (s0 + lax.broadcasted_iota(I32, (n_blk, qn), 1)) // SEL_BLOCK
    forced = (blk == 0) | (blk == cur) | (blk == cur - 1)
    causal = blk <= cur
    imp = jnp.where(causal, imp + jnp.where(forced, FORCE_BONUS, 0.0), -jnp.inf)
    rank = jnp.zeros((n_blk, qn), F32)
    for i in range(n_blk):
        vi = imp[i:i + 1, :]
        ahead = (vi > imp) | ((vi == imp) & (blk > i))
        rank = rank + jnp.where(ahead, 1.0, 0.0)
    sel_bias = jnp.where((rank < n_sel) & causal, 0.0, NEG).astype(BF16)
    q_aug = jnp.concatenate([q_t, tile(sel_bias)], axis=0)

    kstep = ROW_TILE
    c_last = s0 // kstep

    def sel_scores(c):
        k0 = pl.multiple_of(c * kstep, kstep)
        return _dot(ks_ref[pl.ds(k0, kstep), :], q_aug)

    def sel_values(c):
        return jnp.concatenate([vst_ref[0, c], ones_rows], axis=0)

    key = c_last * kstep + lax.broadcasted_iota(I32, (kstep, qn), 0)
    future = jnp.where(key <= s0 + lax.broadcasted_iota(I32, (kstep, qn), 1), 0.0, NEG)
    s_last = sel_scores(c_last) + tile(future)
    m_s = jnp.max(s_last, axis=0, keepdims=True)
    acc_s = _dot(sel_values(c_last), jnp.exp(s_last - m_s).astype(BF16))

    def sel_step(c, carry):
        m_i, acc = carry
        s = sel_scores(c)
        m_new = jnp.maximum(m_i, jnp.max(s, axis=0, keepdims=True))
        p = jnp.exp(s - m_new).astype(BF16)
        return m_new, jnp.exp(m_i - m_new) * acc + _dot(sel_values(c), p)

    _, acc_s = lax.fori_loop(0, c_last, sel_step, (m_s, acc_s))
    o_sel = acc_s[0:HEAD_DIM] * (1.0 / acc_s[HEAD_DIM:HEAD_DIM + 1])

    zeros = jnp.zeros_like(q_t)
    q_win = jnp.where(head == 0, jnp.concatenate([q_t, zeros], axis=0), jnp.concatenate([zeros, q_t], axis=0))
    n_wb = WINDOW // qn + 1
    sw = []
    for kb in range(n_wb):
        bi = qb - (n_wb - 1) + kb
        k0 = pl.multiple_of(jnp.maximum(bi, 0) * qn, qn)
        pos = bi * qn + lax.broadcasted_iota(I32, (qn, qn), 0)
        tq = s0 + lax.broadcasted_iota(I32, (qn, qn), 1)
        ok = (pos >= 0) & (pos <= tq) & (pos > tq - WINDOW)
        sw.append(_dot(kw_ref[pl.ds(k0, qn), :], q_win) + tile(jnp.where(ok, 0.0, NEG)))
    m_w = jnp.max(sw[0], axis=0, keepdims=True)
    for kb in range(1, n_wb):
        m_w = jnp.maximum(m_w, jnp.max(sw[kb], axis=0, keepdims=True))
    acc_w = jnp.zeros((HEAD_DIM + 16, lanes), F32)
    for kb in range(n_wb):
        vt = jnp.concatenate([vwt_ref[0, jnp.maximum(qb - (n_wb - 1) + kb, 0)], ones_rows[:, 0:qn]], axis=0)
        acc_w = acc_w + _dot(vt, jnp.exp(sw[kb] - m_w).astype(BF16))
    o_win = acc_w[0:HEAD_DIM] * (1.0 / acc_w[HEAD_DIM:HEAD_DIM + 1])

    gate = _sigmoid(gl_ref[0, 0, 0])
    o = o_cmp * gate[0:1] + o_sel * gate[1:2] + o_win * gate[2:3]
    o4 = jnp.concatenate([o[:, i * qn:(i + 1) * qn] for i in range(g)], axis=0)
    o_ref[...] = o4.T.astype(o_ref.dtype)


def _overlap_t(n_c, n_sb):
    c0 = jnp.arange(n_c)[None, :] * CMP_STRIDE
    b0 = jnp.arange(HEAD_DIM)[:, None] * SEL_BLOCK
    ov = jnp.minimum(c0 + CMP_LEN, b0 + SEL_BLOCK) - jnp.maximum(c0, b0)
    ov = (jnp.maximum(ov, 0) / CMP_STRIDE).astype(F32)
    return jnp.where(jnp.arange(HEAD_DIM)[:, None] < n_sb, ov, 0.0)


def _nsa_attention(q, kc, vct, ks, vst, kw, vwt, gl, ov_t, b, seq):
    n_c = seq // CMP_STRIDE
    n_sb = seq // SEL_BLOCK
    n_qb = seq // Q_BLOCK
    n_sel = min(N_SEL, n_sb)
    t = b * seq
    width = NSA_GROUP * HEAD_DIM
    vst4 = vst.reshape(b, seq // ROW_TILE, LANES, ROW_TILE)
    vwt4 = vwt.reshape(b, n_qb, LANES, Q_BLOCK)
    return pl.pallas_call(
        functools.partial(_nsa_kernel, seq=seq, n_sel=n_sel),
        grid=(b, NSA_KV_HEADS, n_qb),
        in_specs=[pl.BlockSpec((Q_BLOCK, width), lambda bi, h, i: (bi * n_qb + i, h)),
                  pl.BlockSpec((1, 1, n_c, HEAD_DIM), lambda bi, h, i: (0, bi * NSA_KV_HEADS + h, 0, 0)),
                  pl.BlockSpec((1, 1, HEAD_DIM, n_c), lambda bi, h, i: (1, bi * NSA_KV_HEADS + h, 0, 0)),
                  pl.BlockSpec((seq, LANES), lambda bi, h, i: (bi, h)),
                  pl.BlockSpec((1, seq // ROW_TILE, HEAD_DIM, ROW_TILE), lambda bi, h, i: (bi, 0, h, 0)),
                  pl.BlockSpec((seq, LANES), lambda bi, h, i: (bi, 0)),
                  pl.BlockSpec((1, n_qb, HEAD_DIM, Q_BLOCK), lambda bi, h, i: (bi, 0, h, 0)),
                  pl.BlockSpec((1, 1, 1, 3, NSA_GROUP * Q_BLOCK), lambda bi, h, i: (bi, h, i, 0, 0)),
                  pl.BlockSpec((HEAD_DIM, n_c), lambda bi, h, i: (0, 0))],
        out_specs=pl.BlockSpec((Q_BLOCK, width), lambda bi, h, i: (bi * n_qb + i, h)),
        out_shape=jax.ShapeDtypeStruct((t, NSA_WIDTH), BF16),
        compiler_params=_params("parallel", "parallel", "arbitrary"),
        name="nsa_attention",
    )(q, kc, vct, ks, vst4, kw, vwt4, gl, ov_t)


def _gdn_conv_kernel(cur_ref, prev_ref, w_ref, o_ref, ext_ref):
    ts = cur_ref.shape[1]
    halo = prev_ref.shape[1]
    first = pl.program_id(1) == 0
    ext_ref[0:halo, :] = jnp.where(first, 0.0, prev_ref[0])
    ext_ref[halo:halo + ts, :] = cur_ref[0]
    acc = cur_ref[0] * w_ref[GDN_CONV - 1:GDN_CONV, :]
    for k in range(GDN_CONV - 1):
        acc = acc + ext_ref[pl.ds(halo - (GDN_CONV - 1) + k, ts), :] * w_ref[k:k + 1, :]
    o_ref[0] = _silu(acc)


def _gdn_conv(x3, w):
    b, seq, ch = x3.shape
    ts = min(ROW_TILE, seq)
    halo = 8
    per_tile = ts // halo
    return pl.pallas_call(
        _gdn_conv_kernel,
        grid=(b, seq // ts),
        in_specs=[pl.BlockSpec((1, ts, ch), lambda bi, i: (bi, i, 0)),
                  pl.BlockSpec((1, halo, ch), lambda bi, i: (bi, jnp.maximum(i * per_tile - 1, 0), 0)),
                  pl.BlockSpec((GDN_CONV, ch), lambda bi, i: (0, 0))],
        out_specs=pl.BlockSpec((1, ts, ch), lambda bi, i: (bi, i, 0)),
        out_shape=jax.ShapeDtypeStruct((b, seq, ch), F32),
        scratch_shapes=[pltpu.VMEM((ts + halo, ch), F32)],
        compiler_params=_params("parallel", "parallel"),
        name="gdn_conv",
    )(x3, x3, w)


def _l2norm(v):
    return v * lax.rsqrt(jnp.sum(v * v, axis=-1, keepdims=True) + 1e-6)


def _softplus(v):
    return jnp.maximum(v, 0.0) + jnp.log1p(jnp.exp(-jnp.abs(v)))


def _gdn_kernel(q_ref, k_ref, v_ref, z_ref, b_ref, ac_ref, ar_ref, alog_ref, dtb_ref, nw_ref, o_ref, state_ref,
                *, chunks):
    c_len = GDN_CHUNK

    @pl.when(pl.program_id(1) == 0)
    def _():
        state_ref[...] = jnp.zeros_like(state_ref)

    ri = lax.broadcasted_iota(I32, (c_len, c_len), 0)
    ci = lax.broadcasted_iota(I32, (c_len, c_len), 1)
    lower = (ri >= ci)[None]
    strict = (ri > ci)[None]
    upper = (ri <= ci)[None]
    eye = jnp.where(ri == ci, 1.0, 0.0)[None]
    rate = -jnp.exp(alog_ref[...])
    dtb = dtb_ref[...]
    nw = nw_ref[...]

    def chunk(c, carry):
        r0 = pl.multiple_of(c * c_len, c_len)
        rows = pl.ds(r0, c_len)
        q = _l2norm(q_ref[0, 0, :, rows, :]) * HEAD_DIM ** -0.5
        k = _l2norm(k_ref[0, 0, :, rows, :])
        v = v_ref[0, 0, :, rows, :]
        beta = _sigmoid(b_ref[0, :, rows, :])
        g_col = rate * _softplus(ac_ref[0, :, rows, :] + dtb)
        g_row = rate * _softplus(ar_ref[0, :, c] + dtb)
        gc_col = jnp.sum(jnp.where(lower, g_row, 0.0), axis=2, keepdims=True)
        gc_row = jnp.sum(jnp.where(upper, g_col, 0.0), axis=1, keepdims=True)
        decay = jnp.exp(jnp.where(lower, gc_col - gc_row, -jnp.inf))
        kb = k * beta
        a = jnp.where(strict, _bmm_nt(kb, k) * decay, 0.0)
        inv = eye - a
        pw = a
        for _ in range(int(np.log2(c_len)) - 1):
            pw = _bmm(pw, pw)
            inv = inv + _bmm(inv, pw)
        e_col = jnp.exp(gc_col)
        u = _bmm(inv, v * beta)
        w = _bmm(inv, kb * e_col)
        qk = _bmm_nt(q, k) * decay
        g_last = gc_col[:, c_len - 1:c_len, :]
        k_dec = k * jnp.exp(g_last - gc_col)
        state = state_ref[...]
        v_new = u - _bmm(w, state)
        o = _bmm(q * e_col, state) + _bmm(qk, v_new)
        state_ref[...] = state * jnp.exp(g_last) + _bmm(jnp.swapaxes(k_dec, 1, 2), v_new)
        o = o * lax.rsqrt(jnp.mean(o * o, axis=-1, keepdims=True) + 1e-6) * nw
        o_ref[0, :, rows, :] = (o * _silu(z_ref[0, :, rows, :])).astype(o_ref.dtype)
        return carry

    lax.fori_loop(0, chunks, chunk, 0)


def _gdn(qkvh, zh, b_col, a_col, a_row, a_log, dt_bias, norm_w):
    _, b, h, seq, dh = qkvh.shape
    chunks = min(8, seq // GDN_CHUNK)
    rows = chunks * GDN_CHUNK

    def qkv_spec(r):
        return pl.BlockSpec((1, 1, h, rows, dh), lambda bi, i: (r, bi, 0, i, 0))

    col = pl.BlockSpec((1, h, rows, 1), lambda bi, i: (bi, 0, i, 0))
    head = pl.BlockSpec((h, 1, 1), lambda bi, i: (0, 0, 0))
    return pl.pallas_call(
        functools.partial(_gdn_kernel, chunks=chunks),
        grid=(b, seq // rows),
        in_specs=[qkv_spec(0), qkv_spec(1), qkv_spec(2),
                  pl.BlockSpec((1, h, rows, dh), lambda bi, i: (bi, 0, i, 0)),
                  col, col,
                  pl.BlockSpec((1, h, chunks, 1, GDN_CHUNK), lambda bi, i: (bi, 0, i, 0, 0)),
                  head, head,
                  pl.BlockSpec((1, dh), lambda bi, i: (0, 0))],
        out_specs=pl.BlockSpec((1, h, rows, dh), lambda bi, i: (bi, 0, i, 0)),
        out_shape=jax.ShapeDtypeStruct((b, h, seq, dh), BF16),
        scratch_shapes=[pltpu.VMEM((h, dh, dh), F32)],
        compiler_params=_params("parallel", "arbitrary"),
        name="gdn_delta_rule",
    )(qkvh, qkvh, qkvh, zh, b_col, a_col, a_row, a_log, dt_bias, norm_w)


def _conf_kernel(cur_ref, prev_ref, w_ref, b_ref, lnw_ref, lnb_ref, o_ref, ext_ref):
    ts = cur_ref.shape[1]
    halo = prev_ref.shape[1]
    first = pl.program_id(1) == 0

    def glu(blk):
        return blk[:, :CONF_CH] * _sigmoid(blk[:, CONF_CH:])

    ext_ref[0:halo, :] = jnp.where(first, 0.0, glu(prev_ref[0]))
    ext_ref[halo:halo + ts, :] = glu(cur_ref[0])
    acc = jnp.zeros((ts, CONF_CH), F32) + b_ref[...]
    for k in range(CONF_WIDTH):
        acc = acc + ext_ref[pl.ds(halo - (CONF_WIDTH - 1) + k, ts), :] * w_ref[k:k + 1, :]
    o_ref[0] = _silu(_layer_norm(acc, lnw_ref[...], lnb_ref[...])).astype(o_ref.dtype)


def _conformer(x3, w, bias, ln_w, ln_b):
    b, seq, ch = x3.shape
    ts = min(ROW_TILE, seq)
    halo = 32
    per_tile = ts // halo
    vec = pl.BlockSpec((1, CONF_CH), lambda bi, i: (0, 0))
    return pl.pallas_call(
        _conf_kernel,
        grid=(b, seq // ts),
        in_specs=[pl.BlockSpec((1, ts, ch), lambda bi, i: (bi, i, 0)),
                  pl.BlockSpec((1, halo, ch), lambda bi, i: (bi, jnp.maximum(i * per_tile - 1, 0), 0)),
                  pl.BlockSpec((CONF_WIDTH, CONF_CH), lambda bi, i: (0, 0)), vec, vec, vec],
        out_specs=pl.BlockSpec((1, ts, CONF_CH), lambda bi, i: (bi, i, 0)),
        out_shape=jax.ShapeDtypeStruct((b, seq, CONF_CH), BF16),
        scratch_shapes=[pltpu.VMEM((ts + halo, CONF_CH), F32)],
        compiler_params=_params("parallel", "parallel"),
        name="conformer_conv",
    )(x3, x3, w, bias, ln_w, ln_b)


def _outproj_kernel(yn_ref, yg_ref, yc_ref, x_ref, w_ref, lnw_ref, lnb_ref, wr_ref, br_ref, x1_ref, x1b_ref, lg_ref):
    mix = _dot(yn_ref[...], w_ref[0:NSA_WIDTH, :])
    mix = mix + _dot(yg_ref[...], w_ref[NSA_WIDTH:NSA_WIDTH + GDN_WIDTH, :])
    mix = mix + _dot(yc_ref[...], w_ref[NSA_WIDTH + GDN_WIDTH:, :])
    x1 = _layer_norm(DEEPNORM_ALPHA * x_ref[...] + mix, lnw_ref[...], lnb_ref[...])
    x1_ref[...] = x1
    xb = x1.astype(BF16)
    x1b_ref[...] = xb
    lg_ref[...] = _dot(xb, wr_ref[...]) + br_ref[...]


def _outproj(y_nsa, y_gdn, y_conf, x2, w, ln_w, ln_b, wr, br):
    t = x2.shape[0]
    tm = min(ROW_TILE, t)
    row = lambda i: (i, 0)
    fixed = lambda i: (0, 0)
    return pl.pallas_call(
        _outproj_kernel,
        grid=(t // tm,),
        in_specs=[pl.BlockSpec((tm, NSA_WIDTH), row), pl.BlockSpec((tm, GDN_WIDTH), row),
                  pl.BlockSpec((tm, CONF_CH), row), pl.BlockSpec((tm, D_MODEL), row),
                  pl.BlockSpec((D_MODEL, D_MODEL), fixed), pl.BlockSpec((1, D_MODEL), fixed),
                  pl.BlockSpec((1, D_MODEL), fixed), pl.BlockSpec((D_MODEL, LANES), fixed),
                  pl.BlockSpec((1, LANES), fixed)],
        out_specs=[pl.BlockSpec((tm, D_MODEL), row), pl.BlockSpec((tm, D_MODEL), row), pl.BlockSpec((tm, LANES), row)],
        out_shape=[jax.ShapeDtypeStruct((t, D_MODEL), F32), jax.ShapeDtypeStruct((t, D_MODEL), BF16),
                   jax.ShapeDtypeStruct((t, LANES), F32)],
        compiler_params=_params("parallel"),
        name="outproj_ln_router",
    )(y_nsa, y_gdn, y_conf, x2, w, ln_w, ln_b, wr, br)


def _route_kernel(lg_ref, route_ref, counts_ref, carry_ref):
    tm = lg_ref.shape[0]

    @pl.when(pl.program_id(0) == 0)
    def _():
        carry_ref[...] = jnp.zeros_like(carry_ref)

    lg = lg_ref[...]
    lane = lax.broadcasted_iota(I32, (tm, LANES), 1)
    lane_f = lane.astype(F32)

    def top1(vals):
        best = jnp.max(vals, axis=1, keepdims=True)
        idx = jnp.min(jnp.where(vals == best, lane_f, float(LANES)), axis=1, keepdims=True)
        return best, idx

    gl = jnp.where(lane < N_GROUPS, lg, -jnp.inf)
    g_best, g_idx = top1(gl)
    g_w = 1.0 / jnp.sum(jnp.exp(gl - g_best), axis=1, keepdims=True)
    lo = N_GROUPS + g_idx * EXPERTS_PER_GROUP
    el = jnp.where((lane_f >= lo) & (lane_f < lo + EXPERTS_PER_GROUP), lg, -jnp.inf)
    e1, i1 = top1(el)
    e2, i2 = top1(jnp.where(lane_f == i1, -jnp.inf, el))
    r = jnp.exp(e2 - e1)
    w1 = g_w / (1.0 + r)
    w2 = g_w * r / (1.0 + r)
    hit1 = lane_f == i1
    hit2 = lane_f == i2
    onehot = jnp.where(hit1 | hit2, 1.0, 0.0)
    ri = lax.broadcasted_iota(I32, (tm, tm), 0)
    ci = lax.broadcasted_iota(I32, (tm, tm), 1)
    before = jnp.where(ri > ci, 1.0, 0.0).astype(BF16)
    seen = carry_ref[...] + _dot(before, onehot.astype(BF16))
    p1 = jnp.sum(jnp.where(hit1, seen, 0.0), axis=1, keepdims=True)
    p2 = jnp.sum(jnp.where(hit2, seen, 0.0), axis=1, keepdims=True)
    total = carry_ref[...] + jnp.sum(onehot, axis=0, keepdims=True)
    carry_ref[...] = total
    counts_ref[...] = total
    cols = (i1 - N_GROUPS, i2 - N_GROUPS, w1, w2, p1, p2)
    out = jnp.zeros((tm, LANES), F32)
    for j, cval in enumerate(cols):
        out = jnp.where(lane == j, cval, out)
    route_ref[...] = out


def _route(logits):
    t = logits.shape[0]
    tm = min(ROW_TILE, t)
    return pl.pallas_call(
        _route_kernel,
        grid=(t // tm,),
        in_specs=[pl.BlockSpec((tm, LANES), lambda i: (i, 0))],
        out_specs=[pl.BlockSpec((tm, LANES), lambda i: (i, 0)), pl.BlockSpec((1, LANES), lambda i: (0, 0))],
        out_shape=[jax.ShapeDtypeStruct((t, LANES), F32), jax.ShapeDtypeStruct((1, LANES), F32)],
        scratch_shapes=[pltpu.VMEM((1, LANES), F32)],
        compiler_params=_params("arbitrary"),
        name="moe_route",
    )(logits)


def _expert_kernel(be_ref, nu_ref, x_ref, wg_ref, wu_ref, wd_ref, o_ref, wgb_ref, wub_ref, wdb_ref):
    i = pl.program_id(0)
    changed = (i == 0) | (be_ref[i] != be_ref[jnp.maximum(i - 1, 0)])

    @pl.when(changed)
    def _():
        wgb_ref[...] = wg_ref[0, 0].astype(BF16)
        wub_ref[...] = wu_ref[0, 0].astype(BF16)
        wdb_ref[...] = wd_ref[0, 0].astype(BF16)

    @pl.when(i < nu_ref[0])
    def _():
        xb = x_ref[...]
        hid = _silu(_dot(xb, wgb_ref[...])) * _dot(xb, wub_ref[...])
        o_ref[...] = _dot(hid.astype(BF16), wdb_ref[...])

    @pl.when(i >= nu_ref[0])
    def _():
        o_ref[...] = jnp.zeros_like(o_ref)


def _experts(blk_expert, n_used, buf, w_gate, w_up, w_down, layer):
    rows = buf.shape[0]
    n_blocks = rows // MOE_ROWS
    wspec_in = pl.BlockSpec((1, 1, D_MODEL, EXPERT_FF), lambda i, be, nu: (layer, be[i], 0, 0))
    wspec_out = pl.BlockSpec((1, 1, EXPERT_FF, D_MODEL), lambda i, be, nu: (layer, be[i], 0, 0))
    return pl.pallas_call(
        _expert_kernel,
        grid_spec=pltpu.PrefetchScalarGridSpec(
            num_scalar_prefetch=2,
            grid=(n_blocks,),
            in_specs=[pl.BlockSpec((MOE_ROWS, D_MODEL), lambda i, be, nu: (i, 0)), wspec_in, wspec_in, wspec_out],
            out_specs=pl.BlockSpec((MOE_ROWS, D_MODEL), lambda i, be, nu: (i, 0)),
            scratch_shapes=[pltpu.VMEM((D_MODEL, EXPERT_FF), BF16), pltpu.VMEM((D_MODEL, EXPERT_FF), BF16),
                            pltpu.VMEM((EXPERT_FF, D_MODEL), BF16)]),
        out_shape=jax.ShapeDtypeStruct((rows, D_MODEL), F32),
        compiler_params=_params("arbitrary"),
        name="moe_experts",
    )(blk_expert, n_used, buf, w_gate, w_up, w_down)


def _combine_kernel(x1_ref, y0_ref, y1_ref, route_ref, lnw_ref, lnb_ref, o_ref):
    route = route_ref[...]
    moe = y0_ref[...] * route[:, 2:3] + y1_ref[...] * route[:, 3:4]
    o_ref[...] = _layer_norm(DEEPNORM_ALPHA * x1_ref[...] + moe, lnw_ref[...], lnb_ref[...])


def _combine(x1, y0, y1, route, ln_w, ln_b):
    t = x1.shape[0]
    tm = min(ROW_TILE, t)
    row = lambda i: (i, 0)
    fixed = lambda i: (0, 0)
    big = pl.BlockSpec((tm, D_MODEL), row)
    return pl.pallas_call(
        _combine_kernel,
        grid=(t // tm,),
        in_specs=[big, big, big, pl.BlockSpec((tm, LANES), row),
                  pl.BlockSpec((1, D_MODEL), fixed), pl.BlockSpec((1, D_MODEL), fixed)],
        out_specs=big,
        out_shape=jax.ShapeDtypeStruct((t, D_MODEL), F32),
        compiler_params=_params("parallel"),
        name="moe_combine_ln",
    )(x1, y0, y1, route, ln_w, ln_b)


def _mix_heads(x2, l, b, seq, tables, w_in, nsa_cmp_pe, nsa_cmp_w1, nsa_cmp_w2, gdn_conv_w, gdn_a_log, gdn_dt_bias,
               gdn_norm_w, conf_dw_w, conf_dw_b, conf_ln_w, conf_ln_b):
    cos, sa, sb, oh, ov_t = tables
    t = b * seq
    n_c = seq // CMP_STRIDE
    n_gc = seq // GDN_CHUNK
    n_qb = seq // Q_BLOCK
    half = CMP_LEN // 2
    feat = CMP_STRIDE * HEAD_DIM
    q, kvc, ks, kw, vst, vwt, gqkv, gz, conf_in, small = _inproj(x2, _permute_w_in(w_in[l]), cos, sa, sb, oh, seq)

    chunks = kvc.reshape(b, seq, 2, NSA_KV_HEADS, HEAD_DIM).transpose(2, 0, 3, 1, 4)
    chunks = chunks.reshape(2, b * NSA_KV_HEADS, n_c, feat)
    pe = nsa_cmp_pe[l]
    kc, vct = _compress(chunks, pe[:, :half].reshape(2, 1, feat), pe[:, half:].reshape(2, 1, feat),
                        nsa_cmp_w1[l][:, :feat].astype(BF16), nsa_cmp_w1[l][:, feat:].astype(BF16),
                        nsa_cmp_w2[l].astype(BF16))
    gl = small[:, :3 * NSA_HEADS].reshape(b, n_qb, Q_BLOCK, NSA_KV_HEADS, NSA_GROUP, 3)
    gl = gl.transpose(0, 3, 1, 5, 4, 2).reshape(b, NSA_KV_HEADS, n_qb, 3, NSA_GROUP * Q_BLOCK)
    y_nsa = _nsa_attention(q, kc, vct, ks, vst, kw, vwt, gl, ov_t, b, seq)

    conv = _gdn_conv(gqkv.reshape(b, seq, 3 * GDN_WIDTH), gdn_conv_w[l])
    qkvh = conv.reshape(b, seq, 3, GDN_HEADS, HEAD_DIM).transpose(2, 0, 3, 1, 4)
    zh = gz.reshape(b, seq, GDN_HEADS, HEAD_DIM).transpose(0, 2, 1, 3)
    o_b = 3 * NSA_HEADS
    b_t = small[:, o_b:o_b + GDN_HEADS].reshape(b, seq, GDN_HEADS).transpose(0, 2, 1)
    a_t = small[:, o_b + GDN_HEADS:o_b + 2 * GDN_HEADS].reshape(b, seq, GDN_HEADS).transpose(0, 2, 1)
    y_gdn = _gdn(qkvh, zh, b_t[..., None], a_t[..., None], a_t.reshape(b, GDN_HEADS, n_gc, 1, GDN_CHUNK),
                 gdn_a_log[l].reshape(GDN_HEADS, 1, 1), gdn_dt_bias[l].reshape(GDN_HEADS, 1, 1),
                 gdn_norm_w[l].reshape(1, HEAD_DIM))
    y_gdn = y_gdn.transpose(0, 2, 1, 3).reshape(t, GDN_WIDTH)

    y_conf = _conformer(conf_in.reshape(b, seq, 2 * CONF_CH), conf_dw_w[l], conf_dw_b[l].reshape(1, CONF_CH),
                        conf_ln_w[l].reshape(1, CONF_CH), conf_ln_b[l].reshape(1, CONF_CH)).reshape(t, CONF_CH)
    return y_nsa, y_gdn, y_conf


def _moe_block(x1, x1b, logits, l, ln_w, ln_b, moe_w_gate, moe_w_up, moe_w_down):
    t = x1.shape[0]
    tk = 2 * t
    n_blocks = tk // MOE_ROWS + N_EXPERTS
    route, counts = _route(logits)
    eid = route[:, 0:2].astype(I32)
    pos = route[:, 4:6].astype(I32)
    cnt = counts[0, N_GROUPS:N_GROUPS + N_EXPERTS].astype(I32)
    padded = (cnt + MOE_ROWS - 1) // MOE_ROWS * MOE_ROWS
    pad_end = jnp.cumsum(padded)
    pad_start = pad_end - padded
    dest = pad_start[eid] + pos
    src = jnp.zeros((n_blocks * MOE_ROWS,), I32).at[dest.reshape(tk)].set(jnp.arange(tk, dtype=I32) // 2)
    blk_row = jnp.arange(n_blocks, dtype=I32)[:, None] * MOE_ROWS
    blk_expert = jnp.minimum(jnp.sum((pad_end[None, :] <= blk_row).astype(I32), axis=1), N_EXPERTS - 1)
    n_used = (pad_end[-1:] // MOE_ROWS).astype(I32)
    y_buf = _experts(blk_expert, n_used, x1b[src], moe_w_gate, moe_w_up, moe_w_down, l)
    return _combine(x1, y_buf[dest[:, 0]], y_buf[dest[:, 1]], route, ln_w, ln_b)


def kernel(x, w_in, w_out, nsa_cmp_pe, nsa_cmp_w1, nsa_cmp_w2, gdn_conv_w, gdn_a_log, gdn_dt_bias, gdn_norm_w, conf_dw_w, conf_dw_b, conf_ln_w, conf_ln_b, ln1_w, ln1_b, ln2_w, ln2_b, moe_w_group, moe_b_group, moe_w_expert, moe_b_expert, moe_w_gate, moe_w_up, moe_w_down):
    b, seq, d = x.shape
    depth = w_in.shape[0]
    assert d == D_MODEL and seq % ROW_TILE == 0 and seq >= WINDOW + Q_BLOCK
    t = b * seq
    assert seq // SEL_BLOCK <= HEAD_DIM
    tables = _rope_tables(seq) + (_block_onehot(seq), _overlap_t(seq // CMP_STRIDE, seq // SEL_BLOCK))
    w_out_b = w_out.astype(BF16)
    n_pad = LANES - N_GROUPS - N_EXPERTS
    x2 = x.reshape(t, d)
    for l in range(depth):
        y_nsa, y_gdn, y_conf = _mix_heads(x2, l, b, seq, tables, w_in, nsa_cmp_pe, nsa_cmp_w1, nsa_cmp_w2, gdn_conv_w,
                                          gdn_a_log, gdn_dt_bias, gdn_norm_w, conf_dw_w, conf_dw_b, conf_ln_w, conf_ln_b)
        wr = jnp.concatenate([moe_w_group[l], moe_w_expert[l], jnp.zeros((d, n_pad), F32)], axis=1).astype(BF16)
        br = jnp.concatenate([moe_b_group[l], moe_b_expert[l], jnp.zeros((n_pad,), F32)]).reshape(1, LANES)
        x1, x1b, logits = _outproj(y_nsa, y_gdn, y_conf, x2, w_out_b[l], ln1_w[l].reshape(1, d),
                                   ln1_b[l].reshape(1, d), wr, br)
        x2 = _moe_block(x1, x1b, logits, l, ln2_w[l].reshape(1, d), ln2_b[l].reshape(1, d),
                        moe_w_gate, moe_w_up, moe_w_down)
    return x2.reshape(b, seq, d)
```

```python
import functools

import jax
import jax.numpy as jnp
import numpy as np
from jax import lax
from jax.experimental import pallas as pl
from jax.experimental.pallas import tpu as pltpu

F32 = jnp.float32
BF16 = jnp.bfloat16
I32 = jnp.int32

D_MODEL = 1024
DEPTH = 4
HEAD_DIM = 64
NSA_HEADS = 8
NSA_KV_HEADS = 2
NSA_GROUP = NSA_HEADS // NSA_KV_HEADS
CMP_LEN = 32
CMP_STRIDE = 16
CMP_HIDDEN = 128
SEL_BLOCK = 64
N_SEL = 16
WINDOW = 512
Q_BLOCK = 128
FORCE_BONUS = 1.0e4
ROPE_THETA = 500000.0
ROPE_DIM = HEAD_DIM // 4
GDN_HEADS = 4
GDN_CONV = 4
GDN_CHUNK = 64
CONF_CH = 256
CONF_WIDTH = 31
N_GROUPS = 4
EXPERTS_PER_GROUP = 8
N_EXPERTS = N_GROUPS * EXPERTS_PER_GROUP
EXPERT_FF = 512
NSA_WIDTH = NSA_HEADS * HEAD_DIM
KV_WIDTH = NSA_KV_HEADS * HEAD_DIM
GDN_WIDTH = GDN_HEADS * HEAD_DIM
DEEPNORM_ALPHA = (2.0 * DEPTH) ** 0.25
LN_EPS = 1e-5

LANES = 128
V7X_VMEM_BYTES = 64 * 1024 * 1024
VMEM_LIMIT = V7X_VMEM_BYTES * 3 // 4

Q_OFF = 0
KV_OFF = Q_OFF + NSA_WIDTH
GQKV_OFF = KV_OFF + 6 * KV_WIDTH
GZ_OFF = GQKV_OFF + 3 * GDN_WIDTH
CONF_OFF = GZ_OFF + GDN_WIDTH
SMALL_OFF = CONF_OFF + 2 * CONF_CH
IN_COLS = SMALL_OFF + LANES
N_SMALL = 3 * NSA_HEADS + 2 * GDN_HEADS

ROW_TILE = 512
MOE_ROWS = 512
GDN_BLOCK_CHUNKS = 4
GDN_BLOCK_BATCH = 4
SEL_KEYS = 512
NEG = -1e30


def _params(*sem):
    return pltpu.CompilerParams(dimension_semantics=sem, vmem_limit_bytes=VMEM_LIMIT)


def _dot(a, b):
    return jnp.dot(a, b, preferred_element_type=F32)


def _dot_nt(a, b):
    return lax.dot_general(a, b, (((1,), (1,)), ((), ())), preferred_element_type=F32)


def _bmm(a, b):
    return lax.dot_general(a.astype(BF16), b.astype(BF16), (((2,), (1,)), ((0,), (0,))),
                           preferred_element_type=F32)


def _bmm_nt(a, b):
    return lax.dot_general(a.astype(BF16), b.astype(BF16), (((2,), (2,)), ((0,), (0,))),
                           preferred_element_type=F32)


def _sigmoid(x):
    return 1.0 / (1.0 + jnp.exp(-x))


def _silu(x):
    return x * _sigmoid(x)


def _layer_norm(v, w, b):
    mu = jnp.mean(v, axis=-1, keepdims=True)
    d = v - mu
    var = jnp.mean(d * d, axis=-1, keepdims=True)
    return d * lax.rsqrt(var + LN_EPS) * w + b


def _inproj_kernel(x_ref, w_ref, cos_ref, sa_ref, sb_ref, oh_ref, q_ref, kvc_ref, ks_ref, kw_ref, vst_ref, vwt_ref,
                   gqkv_ref, gz_ref, conf_ref, small_ref):
    xb = x_ref[...].astype(BF16)
    cos = cos_ref[...]
    sa = sa_ref[...]
    sb = sb_ref[...]
    tm = xb.shape[0]

    def rope(h):
        return h * cos + pltpu.roll(h, LANES - ROPE_DIM // 2, 1) * sa + pltpu.roll(h, ROPE_DIM // 2, 1) * sb

    hq = _dot(xb, w_ref[:, Q_OFF:KV_OFF])
    scale = HEAD_DIM ** -0.5
    for j in range(NSA_WIDTH // LANES):
        q_ref[:, j * LANES:(j + 1) * LANES] = (rope(hq[:, j * LANES:(j + 1) * LANES]) * scale).astype(BF16)
    hkv = _dot(xb, w_ref[:, KV_OFF:GQKV_OFF])
    part = lambda j: hkv[:, j * LANES:(j + 1) * LANES]
    kvc_ref[:, 0:LANES] = rope(part(0)).astype(BF16)
    kvc_ref[:, LANES:2 * LANES] = part(1).astype(BF16)
    ksel = rope(part(2))
    first_half = lax.broadcasted_iota(I32, (tm, LANES), 1) < HEAD_DIM
    oh = oh_ref[...]
    ks_ref[:, 0:LANES] = jnp.where(first_half, ksel, oh).astype(BF16)
    ks_ref[:, LANES:2 * LANES] = jnp.where(first_half, pltpu.roll(ksel, HEAD_DIM, 1), oh).astype(BF16)
    kw_ref[...] = rope(part(4)).astype(BF16)
    vst_ref[0] = part(3).T.astype(BF16)
    vwt = part(5).T.astype(BF16)
    for c in range(tm // Q_BLOCK):
        vwt_ref[c] = vwt[:, c * Q_BLOCK:(c + 1) * Q_BLOCK]
    gqkv_ref[...] = _dot(xb, w_ref[:, GQKV_OFF:GZ_OFF])
    gz_ref[...] = _dot(xb, w_ref[:, GZ_OFF:CONF_OFF])
    conf_ref[...] = _dot(xb, w_ref[:, CONF_OFF:SMALL_OFF])
    small_ref[...] = _dot(xb, w_ref[:, SMALL_OFF:IN_COLS])


def _inproj(x2, w, cos, sa, sb, oh, seq):
    t = x2.shape[0]
    tm = ROW_TILE
    n_pos = seq // tm
    row = lambda i: (i, 0)
    pos = lambda i: (i % n_pos, 0)
    widths = (NSA_WIDTH, 2 * LANES, 2 * LANES, LANES, 3 * GDN_WIDTH, GDN_WIDTH, 2 * CONF_CH, LANES)
    dtypes = (BF16, BF16, BF16, BF16, F32, F32, F32, F32)
    flat_specs = [pl.BlockSpec((tm, wd), row) for wd in widths]
    flat_shapes = [jax.ShapeDtypeStruct((t, wd), dt) for wd, dt in zip(widths, dtypes)]
    n_q = tm // Q_BLOCK
    out_specs = flat_specs[:4] + [pl.BlockSpec((1, LANES, tm), lambda i: (i, 0, 0)),
                                  pl.BlockSpec((n_q, LANES, Q_BLOCK), lambda i: (i, 0, 0))] + flat_specs[4:]
    out_shape = flat_shapes[:4] + [jax.ShapeDtypeStruct((t // tm, LANES, tm), BF16),
                                   jax.ShapeDtypeStruct((t // Q_BLOCK, LANES, Q_BLOCK), BF16)] + flat_shapes[4:]
    return pl.pallas_call(
        _inproj_kernel,
        grid=(t // tm,),
        in_specs=[pl.BlockSpec((tm, D_MODEL), row),
                  pl.BlockSpec((D_MODEL, IN_COLS), lambda i: (0, 0)),
                  pl.BlockSpec((tm, LANES), pos), pl.BlockSpec((tm, LANES), pos), pl.BlockSpec((tm, LANES), pos),
                  pl.BlockSpec((tm, LANES), pos)],
        out_specs=out_specs,
        out_shape=out_shape,
        compiler_params=_params("parallel"),
        name="inproj",
    )(x2, w, cos, sa, sb, oh)


def _block_onehot(seq):
    blk = jnp.arange(seq)[:, None] // SEL_BLOCK
    lane = jnp.arange(LANES)[None, :]
    return jnp.where(lane - HEAD_DIM == blk, 1.0, 0.0).astype(F32)


def _rope_tables(seq):
    half = ROPE_DIM // 2
    inv = ROPE_THETA ** (-jnp.arange(0, ROPE_DIM, 2, dtype=F32) / ROPE_DIM)
    ang = jnp.arange(seq, dtype=F32)[:, None] * inv[None, :]
    c, s = jnp.cos(ang), jnp.sin(ang)
    ones = jnp.ones((seq, HEAD_DIM - ROPE_DIM), F32)
    zeros = jnp.zeros((seq, HEAD_DIM - ROPE_DIM), F32)
    zh = jnp.zeros((seq, half), F32)
    cos_h = jnp.concatenate([c, c, ones], axis=1)
    sa_h = jnp.concatenate([-s, zh, zeros], axis=1)
    sb_h = jnp.concatenate([zh, s, zeros], axis=1)
    rep = LANES // HEAD_DIM
    return jnp.tile(cos_h, (1, rep)), jnp.tile(sa_h, (1, rep)), jnp.tile(sb_h, (1, rep))


def _permute_w_in(w):
    o_q, o_kv = 0, NSA_WIDTH
    o_gate = o_kv + 6 * KV_WIDTH
    o_gqkv = o_gate + 3 * NSA_HEADS
    o_gz = o_gqkv + 3 * GDN_WIDTH
    o_b = o_gz + GDN_WIDTH
    o_a = o_b + GDN_HEADS
    o_conf = o_a + GDN_HEADS
    pad = jnp.zeros((w.shape[0], LANES - N_SMALL), w.dtype)
    return jnp.concatenate([w[:, o_q:o_gate], w[:, o_gqkv:o_b], w[:, o_conf:], w[:, o_gate:o_gqkv],
                            w[:, o_b:o_conf], pad], axis=1).astype(BF16)


def _compress_kernel(ch_ref, pet_ref, peb_ref, w1t_ref, w1b_ref, w2_ref, o_ref, ot_ref):
    ch = ch_ref[0, 0].astype(F32)
    top = _dot((ch + pet_ref[0]).astype(BF16), w1t_ref[0])
    bot = _dot((ch + peb_ref[0]).astype(BF16), w1b_ref[0])
    n = bot.shape[0]
    hid = top + pltpu.roll(bot, n - 1, 0)
    out = _dot(_silu(hid).astype(BF16), w2_ref[0])
    o_ref[0, 0] = out.astype(o_ref.dtype)
    ot_ref[0, 0] = jnp.concatenate([out, jnp.zeros_like(out)], axis=1).T[:HEAD_DIM].astype(ot_ref.dtype)


def _compress(chunks, pe_top, pe_bot, w1_top, w1_bot, w2):
    _, bh, n_chunk, feat = chunks.shape
    per = lambda r, i: (r, 0, 0)
    return pl.pallas_call(
        _compress_kernel,
        grid=(2, bh),
        in_specs=[pl.BlockSpec((1, 1, n_chunk, feat), lambda r, i: (r, i, 0, 0)),
                  pl.BlockSpec((1, 1, feat), per), pl.BlockSpec((1, 1, feat), per),
                  pl.BlockSpec((1, feat, CMP_HIDDEN), per), pl.BlockSpec((1, feat, CMP_HIDDEN), per),
                  pl.BlockSpec((1, CMP_HIDDEN, HEAD_DIM), per)],
        out_specs=[pl.BlockSpec((1, 1, n_chunk, HEAD_DIM), lambda r, i: (r, i, 0, 0)),
                   pl.BlockSpec((1, 1, HEAD_DIM, n_chunk), lambda r, i: (r, i, 0, 0))],
        out_shape=[jax.ShapeDtypeStruct((2, bh, n_chunk, HEAD_DIM), BF16),
                   jax.ShapeDtypeStruct((2, bh, HEAD_DIM, n_chunk), BF16)],
        compiler_params=_params("parallel", "parallel"),
        name="nsa_compress",
    )(chunks, pe_top, pe_bot, w1_top, w1_bot, w2)


def _nsa_kernel(q_ref, kc_ref, vct_ref, ks_ref, vst_ref, kw_ref, vwt_ref, gl_ref, ovt_ref, o_ref, *, seq, n_sel):
    g, qn = NSA_GROUP, Q_BLOCK
    lanes = g * qn
    n_c = seq // CMP_STRIDE
    head = pl.program_id(1)
    qb = pl.program_id(2)
    s0 = qb * qn
    tile = lambda a: jnp.concatenate([a] * g, axis=1)
    qt4 = q_ref[...].astype(F32).T
    q_t = jnp.concatenate([qt4[i * HEAD_DIM:(i + 1) * HEAD_DIM] for i in range(g)], axis=1).astype(BF16)
    t_row = s0 + (lax.broadcasted_iota(I32, (1, lanes), 1) & (qn - 1))
    ones_rows = jnp.ones((16, ROW_TILE), BF16)

    sc = _dot(kc_ref[0, 0], q_t)
    cend = lax.broadcasted_iota(I32, (n_c, lanes), 0) * CMP_STRIDE + (CMP_LEN - 1)
    sc = jnp.where(cend <= t_row, sc, -jnp.inf)
    m = jnp.max(sc, axis=0, keepdims=True)
    m = jnp.where(m == -jnp.inf, 0.0, m)
    e = jnp.exp(sc - m)
    pc = e * (1.0 / jnp.maximum(jnp.sum(e, axis=0, keepdims=True), 1e-30))
    o_cmp = _dot(vct_ref[0, 0], pc.astype(BF16))

    pcs = pc[:, 0:qn]
    for i in range(1, g):
        pcs = pcs + pc[:, i * qn:(i + 1) * qn]
    imp = jnp.dot(ovt_ref[...], pcs, preferred_element_type=F32, precision=lax.Precision.HIGHEST)
    n_blk = imp.shape[0]
    blk = lax.broadcasted_iota(I32, (n_blk, qn), 0)
    cur = (s0 + lax.broadcasted_iota(I32, (n_blk, qn), 1)) // SEL_BLOCK
    forced = (blk == 0) | (blk == cur) | (blk == cur - 1)
    causal = blk <= cur
    imp = jnp.where(causal, imp + jnp.where(forced, FORCE_BONUS, 0.0), -jnp.inf)
    rank = jnp.zeros((n_blk, qn), F32)
    for i in range(n_blk):
        vi = imp[i:i + 1, :]
        ahead = (vi > imp) | ((vi == imp) & (blk > i))
        rank = rank + jnp.where(ahead, 1.0, 0.0)
    sel_bias = jnp.where((rank < n_sel) & causal, 0.0, NEG).astype(BF16)
    q_aug = jnp.concatenate([q_t, tile(sel_bias)], axis=0)

    kstep = ROW_TILE
    c_last = s0 // kstep

    def sel_scores(c):
        k0 = pl.multiple_of(c * kstep, kstep)
        return _dot(ks_ref[pl.ds(k0, kstep), :], q_aug)

    def sel_values(c):
        return jnp.concatenate([vst_ref[0, c], ones_rows], axis=0)

    key = c_last * kstep + lax.broadcasted_iota(I32, (kstep, qn), 0)
    future = jnp.where(key <= s0 + lax.broadcasted_iota(I32, (kstep, qn), 1), 0.0, NEG)
    s_last = sel_scores(c_last) + tile(future)
    m_s = jnp.max(s_last, axis=0, keepdims=True)
    acc_s = _dot(sel_values(c_last), jnp.exp(s_last - m_s).astype(BF16))

    def sel_step(c, carry):
        m_i, acc = carry
        s = sel_scores(c)
        m_new = jnp.maximum(m_i, jnp.max(s, axis=0, keepdims=True))
        p = jnp.exp(s - m_new).astype(BF16)
        return m_new, jnp.exp(m_i - m_new) * acc + _dot(sel_values(c), p)

    _, acc_s = lax.fori_loop(0, c_last, sel_step, (m_s, acc_s))
    o_sel = acc_s[0:HEAD_DIM] * (1.0 / acc_s[HEAD_DIM:HEAD_DIM + 1])

    zeros = jnp.zeros_like(q_t)
    q_win = jnp.where(head == 0, jnp.concatenate([q_t, zeros], axis=0), jnp.concatenate([zeros, q_t], axis=0))
    n_wb = WINDOW // qn + 1
    sw = []
    for kb in range(n_wb):
        bi = qb - (n_wb - 1) + kb
        k0 = pl.multiple_of(jnp.maximum(bi, 0) * qn, qn)
        pos = bi * qn + lax.broadcasted_iota(I32, (qn, qn), 0)
        tq = s0 + lax.broadcasted_iota(I32, (qn, qn), 1)
        ok = (pos >= 0) & (pos <= tq) & (pos > tq - WINDOW)
        sw.append(_dot(kw_ref[pl.ds(k0, qn), :], q_win) + tile(jnp.where(ok, 0.0, NEG)))
    m_w = jnp.max(sw[0], axis=0, keepdims=True)
    for kb in range(1, n_wb):
        m_w = jnp.maximum(m_w, jnp.max(sw[kb], axis=0, keepdims=True))
    acc_w = jnp.zeros((HEAD_DIM + 16, lanes), F32)
    for kb in range(n_wb):
        vt = jnp.concatenate([vwt_ref[0, jnp.maximum(qb - (n_wb - 1) + kb, 0)], ones_rows[:, 0:qn]], axis=0)
        acc_w = acc_w + _dot(vt, jnp.exp(sw[kb] - m_w).astype(BF16))
    o_win = acc_w[0:HEAD_DIM] * (1.0 / acc_w[HEAD_DIM:HEAD_DIM + 1])

    gate = _sigmoid(gl_ref[0, 0, 0])
    o = o_cmp * gate[0:1] + o_sel * gate[1:2] + o_win * gate[2:3]
    o4 = jnp.concatenate([o[:, i * qn:(i + 1) * qn] for i in range(g)], axis=0)
    o_ref[...] = o4.T.astype(o_ref.dtype)


def _overlap_t(n_c, n_sb):
    c0 = jnp.arange(n_c)[None, :] * CMP_STRIDE
    b0 = jnp.arange(HEAD_DIM)[:, None] * SEL_BLOCK
    ov = jnp.minimum(c0 + CMP_LEN, b0 + SEL_BLOCK) - jnp.maximum(c0, b0)
    ov = (jnp.maximum(ov, 0) / CMP_STRIDE).astype(F32)
    return jnp.where(jnp.arange(HEAD_DIM)[:, None] < n_sb, ov, 0.0)


def _nsa_attention(q, kc, vct, ks, vst, kw, vwt, gl, ov_t, b, seq):
    n_c = seq // CMP_STRIDE
    n_sb = seq // SEL_BLOCK
    n_qb = seq // Q_BLOCK
    n_sel = min(N_SEL, n_sb)
    t = b * seq
    width = NSA_GROUP * HEAD_DIM
    vst4 = vst.reshape(b, seq // ROW_TILE, LANES, ROW_TILE)
    vwt4 = vwt.reshape(b, n_qb, LANES, Q_BLOCK)
    return pl.pallas_call(
        functools.partial(_nsa_kernel, seq=seq, n_sel=n_sel),
        grid=(b, NSA_KV_HEADS, n_qb),
        in_specs=[pl.BlockSpec((Q_BLOCK, width), lambda bi, h, i: (bi * n_qb + i, h)),
                  pl.BlockSpec((1, 1, n_c, HEAD_DIM), lambda bi, h, i: (0, bi * NSA_KV_HEADS + h, 0, 0)),
                  pl.BlockSpec((1, 1, HEAD_DIM, n_c), lambda bi, h, i: (1, bi * NSA_KV_HEADS + h, 0, 0)),
                  pl.BlockSpec((seq, LANES), lambda bi, h, i: (bi, h)),
                  pl.BlockSpec((1, seq // ROW_TILE, HEAD_DIM, ROW_TILE), lambda bi, h, i: (bi, 0, h, 0)),
                  pl.BlockSpec((seq, LANES), lambda bi, h, i: (bi, 0)),
                  pl.BlockSpec((1, n_qb, HEAD_DIM, Q_BLOCK), lambda bi, h, i: (bi, 0, h, 0)),
                  pl.BlockSpec((1, 1, 1, 3, NSA_GROUP * Q_BLOCK), lambda bi, h, i: (bi, h, i, 0, 0)),
                  pl.BlockSpec((HEAD_DIM, n_c), lambda bi, h, i: (0, 0))],
        out_specs=pl.BlockSpec((Q_BLOCK, width), lambda bi, h, i: (bi * n_qb + i, h)),
        out_shape=jax.ShapeDtypeStruct((t, NSA_WIDTH), BF16),
        compiler_params=_params("parallel", "parallel", "arbitrary"),
        name="nsa_attention",
    )(q, kc, vct, ks, vst4, kw, vwt4, gl, ov_t)


def _gdn_conv_kernel(cur_ref, prev_ref, w_ref, o_ref, ext_ref):
    ts = cur_ref.shape[1]
    halo = prev_ref.shape[1]
    first = pl.program_id(1) == 0
    ext_ref[0:halo, :] = jnp.where(first, 0.0, prev_ref[0])
    ext_ref[halo:halo + ts, :] = cur_ref[0]
    acc = cur_ref[0] * w_ref[GDN_CONV - 1:GDN_CONV, :]
    for k in range(GDN_CONV - 1):
        acc = acc + ext_ref[pl.ds(halo - (GDN_CONV - 1) + k, ts), :] * w_ref[k:k + 1, :]
    o_ref[0] = _silu(acc)


def _gdn_conv(x3, w):
    b, seq, ch = x3.shape
    ts = min(ROW_TILE, seq)
    halo = 8
    per_tile = ts // halo
    return pl.pallas_call(
        _gdn_conv_kernel,
        grid=(b, seq // ts),
        in_specs=[pl.BlockSpec((1, ts, ch), lambda bi, i: (bi, i, 0)),
                  pl.BlockSpec((1, halo, ch), lambda bi, i: (bi, jnp.maximum(i * per_tile - 1, 0), 0)),
                  pl.BlockSpec((GDN_CONV, ch), lambda bi, i: (0, 0))],
        out_specs=pl.BlockSpec((1, ts, ch), lambda bi, i: (bi, i, 0)),
        out_shape=jax.ShapeDtypeStruct((b, seq, ch), F32),
        scratch_shapes=[pltpu.VMEM((ts + halo, ch), F32)],
        compiler_params=_params("parallel", "parallel"),
        name="gdn_conv",
    )(x3, x3, w)


def _l2norm(v):
    return v * lax.rsqrt(jnp.sum(v * v, axis=-1, keepdims=True) + 1e-6)


def _softplus(v):
    return jnp.maximum(v, 0.0) + jnp.log1p(jnp.exp(-jnp.abs(v)))


def _gdn_kernel(q_ref, k_ref, v_ref, z_ref, b_ref, ac_ref, ar_ref, alog_ref, dtb_ref, nw_ref, o_ref, state_ref,
                *, chunks):
    c_len = GDN_CHUNK

    @pl.when(pl.program_id(1) == 0)
    def _():
        state_ref[...] = jnp.zeros_like(state_ref)

    ri = lax.broadcasted_iota(I32, (c_len, c_len), 0)
    ci = lax.broadcasted_iota(I32, (c_len, c_len), 1)
    lower = (ri >= ci)[None]
    strict = (ri > ci)[None]
    upper = (ri <= ci)[None]
    eye = jnp.where(ri == ci, 1.0, 0.0)[None]
    nb, nh = z_ref.shape[0], z_ref.shape[1]
    n = nb * nh
    rate = jnp.concatenate([-jnp.exp(alog_ref[...])] * nb, axis=0)
    dtb = jnp.concatenate([dtb_ref[...]] * nb, axis=0)
    nw = nw_ref[...]

    def chunk(c, carry):
        r0 = pl.multiple_of(c * c_len, c_len)
        rows = pl.ds(r0, c_len)
        q = _l2norm(q_ref[0, :, :, rows, :].reshape(n, c_len, HEAD_DIM)) * HEAD_DIM ** -0.5
        k = _l2norm(k_ref[0, :, :, rows, :].reshape(n, c_len, HEAD_DIM))
        v = v_ref[0, :, :, rows, :].reshape(n, c_len, HEAD_DIM)
        beta = _sigmoid(b_ref[:, :, rows, :].reshape(n, c_len, 1))
        g_col = rate * _softplus(ac_ref[:, :, rows, :].reshape(n, c_len, 1) + dtb)
        g_row = rate * _softplus(ar_ref[:, :, c].reshape(n, 1, c_len) + dtb)
        gc_col = jnp.sum(jnp.where(lower, g_row, 0.0), axis=2, keepdims=True)
        gc_row = jnp.sum(jnp.where(upper, g_col, 0.0), axis=1, keepdims=True)
        decay = jnp.exp(jnp.where(lower, gc_col - gc_row, -jnp.inf))
        kb = k * beta
        a = jnp.where(strict, _bmm_nt(kb, k) * decay, 0.0)
        inv = eye - a
        pw = a
        for _ in range(int(np.log2(c_len)) - 1):
            pw = _bmm(pw, pw)
            inv = inv + _bmm(inv, pw)
        e_col = jnp.exp(gc_col)
        u = _bmm(inv, v * beta)
        w = _bmm(inv, kb * e_col)
        qk = _bmm_nt(q, k) * decay
        g_last = gc_col[:, c_len - 1:c_len, :]
        k_dec = k * jnp.exp(g_last - gc_col)
        state = state_ref[...]
        v_new = u - _bmm(w, state)
        o = _bmm(q * e_col, state) + _bmm(qk, v_new)
        state_ref[...] = state * jnp.exp(g_last) + _bmm(jnp.swapaxes(k_dec, 1, 2), v_new)
        o = o * lax.rsqrt(jnp.mean(o * o, axis=-1, keepdims=True) + 1e-6) * nw
        o = o * _silu(z_ref[:, :, rows, :].reshape(n, c_len, HEAD_DIM))
        o_ref[:, :, rows, :] = o.reshape(nb, nh, c_len, HEAD_DIM).astype(o_ref.dtype)
        return carry

    lax.fori_loop(0, chunks, chunk, 0)


def _gdn(qkvh, zh, b_col, a_col, a_row, a_log, dt_bias, norm_w):
    _, b, h, seq, dh = qkvh.shape
    chunks = GDN_BLOCK_CHUNKS
    rows = chunks * GDN_CHUNK
    nb = GDN_BLOCK_BATCH if b % GDN_BLOCK_BATCH == 0 else 1

    def qkv_spec(r):
        return pl.BlockSpec((1, nb, h, rows, dh), lambda bi, i: (r, bi, 0, i, 0))

    col = pl.BlockSpec((nb, h, rows, 1), lambda bi, i: (bi, 0, i, 0))
    head = pl.BlockSpec((h, 1, 1), lambda bi, i: (0, 0, 0))
    return pl.pallas_call(
        functools.partial(_gdn_kernel, chunks=chunks),
        grid=(b // nb, seq // rows),
        in_specs=[qkv_spec(0), qkv_spec(1), qkv_spec(2),
                  pl.BlockSpec((nb, h, rows, dh), lambda bi, i: (bi, 0, i, 0)),
                  col, col,
                  pl.BlockSpec((nb, h, chunks, 1, GDN_CHUNK), lambda bi, i: (bi, 0, i, 0, 0)),
                  head, head,
                  pl.BlockSpec((1, dh), lambda bi, i: (0, 0))],
        out_specs=pl.BlockSpec((nb, h, rows, dh), lambda bi, i: (bi, 0, i, 0)),
        out_shape=jax.ShapeDtypeStruct((b, h, seq, dh), BF16),
        scratch_shapes=[pltpu.VMEM((nb * h, dh, dh), F32)],
        compiler_params=_params("parallel", "arbitrary"),
        name="gdn_delta_rule",
    )(qkvh, qkvh, qkvh, zh, b_col, a_col, a_row, a_log, dt_bias, norm_w)


def _conf_kernel(cur_ref, prev_ref, w_ref, b_ref, lnw_ref, lnb_ref, o_ref, ext_ref):
    ts = cur_ref.shape[1]
    halo = prev_ref.shape[1]
    first = pl.program_id(1) == 0

    def glu(blk):
        return blk[:, :CONF_CH] * _sigmoid(blk[:, CONF_CH:])

    ext_ref[0:halo, :] = jnp.where(first, 0.0, glu(prev_ref[0]))
    ext_ref[halo:halo + ts, :] = glu(cur_ref[0])
    acc = jnp.zeros((ts, CONF_CH), F32) + b_ref[...]
    for k in range(CONF_WIDTH):
        acc = acc + ext_ref[pl.ds(halo - (CONF_WIDTH - 1) + k, ts), :] * w_ref[k:k + 1, :]
    o_ref[0] = _silu(_layer_norm(acc, lnw_ref[...], lnb_ref[...])).astype(o_ref.dtype)


def _conformer(x3, w, bias, ln_w, ln_b):
    b, seq, ch = x3.shape
    ts = min(ROW_TILE, seq)
    halo = 32
    per_tile = ts // halo
    vec = pl.BlockSpec((1, CONF_CH), lambda bi, i: (0, 0))
    return pl.pallas_call(
        _conf_kernel,
        grid=(b, seq // ts),
        in_specs=[pl.BlockSpec((1, ts, ch), lambda bi, i: (bi, i, 0)),
                  pl.BlockSpec((1, halo, ch), lambda bi, i: (bi, jnp.maximum(i * per_tile - 1, 0), 0)),
                  pl.BlockSpec((CONF_WIDTH, CONF_CH), lambda bi, i: (0, 0)), vec, vec, vec],
        out_specs=pl.BlockSpec((1, ts, CONF_CH), lambda bi, i: (bi, i, 0)),
        out_shape=jax.ShapeDtypeStruct((b, seq, CONF_CH), BF16),
        scratch_shapes=[pltpu.VMEM((ts + halo, CONF_CH), F32)],
        compiler_params=_params("parallel", "parallel"),
        name="conformer_conv",
    )(x3, x3, w, bias, ln_w, ln_b)


def _outproj_kernel(yn_ref, yg_ref, yc_ref, x_ref, w_ref, lnw_ref, lnb_ref, wr_ref, br_ref, x1_ref, x1b_ref, lg_ref):
    mix = _dot(yn_ref[...], w_ref[0:NSA_WIDTH, :])
    mix = mix + _dot(yg_ref[...], w_ref[NSA_WIDTH:NSA_WIDTH + GDN_WIDTH, :])
    mix = mix + _dot(yc_ref[...], w_ref[NSA_WIDTH + GDN_WIDTH:, :])
    x1 = _layer_norm(DEEPNORM_ALPHA * x_ref[...] + mix, lnw_ref[...], lnb_ref[...])
    x1_ref[...] = x1
    xb = x1.astype(BF16)
    x1b_ref[...] = xb
    lg_ref[...] = _dot(xb, wr_ref[...]) + br_ref[...]


def _outproj(y_nsa, y_gdn, y_conf, x2, w, ln_w, ln_b, wr, br):
    t = x2.shape[0]
    tm = min(ROW_TILE, t)
    row = lambda i: (i, 0)
    fixed = lambda i: (0, 0)
    return pl.pallas_call(
        _outproj_kernel,
        grid=(t // tm,),
        in_specs=[pl.BlockSpec((tm, NSA_WIDTH), row), pl.BlockSpec((tm, GDN_WIDTH), row),
                  pl.BlockSpec((tm, CONF_CH), row), pl.BlockSpec((tm, D_MODEL), row),
                  pl.BlockSpec((D_MODEL, D_MODEL), fixed), pl.BlockSpec((1, D_MODEL), fixed),
                  pl.BlockSpec((1, D_MODEL), fixed), pl.BlockSpec((D_MODEL, LANES), fixed),
                  pl.BlockSpec((1, LANES), fixed)],
        out_specs=[pl.BlockSpec((tm, D_MODEL), row), pl.BlockSpec((tm, D_MODEL), row), pl.BlockSpec((tm, LANES), row)],
        out_shape=[jax.ShapeDtypeStruct((t, D_MODEL), F32), jax.ShapeDtypeStruct((t, D_MODEL), BF16),
                   jax.ShapeDtypeStruct((t, LANES), F32)],
        compiler_params=_params("parallel"),
        name="outproj_ln_router",
    )(y_nsa, y_gdn, y_conf, x2, w, ln_w, ln_b, wr, br)


def _route_kernel(lg_ref, route_ref, counts_ref, carry_ref):
    tm = lg_ref.shape[0]

    @pl.when(pl.program_id(0) == 0)
    def _():
        carry_ref[...] = jnp.zeros_like(carry_ref)

    lg = lg_ref[...]
    lane = lax.broadcasted_iota(I32, (tm, LANES), 1)
    lane_f = lane.astype(F32)

    def top1(vals):
        best = jnp.max(vals, axis=1, keepdims=True)
        idx = jnp.min(jnp.where(vals == best, lane_f, float(LANES)), axis=1, keepdims=True)
        return best, idx

    gl = jnp.where(lane < N_GROUPS, lg, -jnp.inf)
    g_best, g_idx = top1(gl)
    g_w = 1.0 / jnp.sum(jnp.exp(gl - g_best), axis=1, keepdims=True)
    lo = N_GROUPS + g_idx * EXPERTS_PER_GROUP
    el = jnp.where((lane_f >= lo) & (lane_f < lo + EXPERTS_PER_GROUP), lg, -jnp.inf)
    e1, i1 = top1(el)
    e2, i2 = top1(jnp.where(lane_f == i1, -jnp.inf, el))
    r = jnp.exp(e2 - e1)
    w1 = g_w / (1.0 + r)
    w2 = g_w * r / (1.0 + r)
    hit1 = lane_f == i1
    hit2 = lane_f == i2
    onehot = jnp.where(hit1 | hit2, 1.0, 0.0)
    ri = lax.broadcasted_iota(I32, (tm, tm), 0)
    ci = lax.broadcasted_iota(I32, (tm, tm), 1)
    before = jnp.where(ri > ci, 1.0, 0.0).astype(BF16)
    seen = carry_ref[...] + _dot(before, onehot.astype(BF16))
    p1 = jnp.sum(jnp.where(hit1, seen, 0.0), axis=1, keepdims=True)
    p2 = jnp.sum(jnp.where(hit2, seen, 0.0), axis=1, keepdims=True)
    total = carry_ref[...] + jnp.sum(onehot, axis=0, keepdims=True)
    carry_ref[...] = total
    counts_ref[...] = total
    cols = (i1 - N_GROUPS, i2 - N_GROUPS, w1, w2, p1, p2)
    out = jnp.zeros((tm, LANES), F32)
    for j, cval in enumerate(cols):
        out = jnp.where(lane == j, cval, out)
    route_ref[...] = out


def _route(logits):
    t = logits.shape[0]
    tm = min(ROW_TILE, t)
    return pl.pallas_call(
        _route_kernel,
        grid=(t // tm,),
        in_specs=[pl.BlockSpec((tm, LANES), lambda i: (i, 0))],
        out_specs=[pl.BlockSpec((tm, LANES), lambda i: (i, 0)), pl.BlockSpec((1, LANES), lambda i: (0, 0))],
        out_shape=[jax.ShapeDtypeStruct((t, LANES), F32), jax.ShapeDtypeStruct((1, LANES), F32)],
        scratch_shapes=[pltpu.VMEM((1, LANES), F32)],
        compiler_params=_params("arbitrary"),
        name="moe_route",
    )(logits)


def _expert_kernel(be_ref, nu_ref, x_ref, wg_ref, wu_ref, wd_ref, o_ref, wgb_ref, wub_ref, wdb_ref):
    i = pl.program_id(0)
    changed = (i == 0) | (be_ref[i] != be_ref[jnp.maximum(i - 1, 0)])

    @pl.when(changed)
    def _():
        wgb_ref[...] = wg_ref[0, 0].astype(BF16)
        wub_ref[...] = wu_ref[0, 0].astype(BF16)
        wdb_ref[...] = wd_ref[0, 0].astype(BF16)

    @pl.when(i < nu_ref[0])
    def _():
        xb = x_ref[...]
        hid = _silu(_dot(xb, wgb_ref[...])) * _dot(xb, wub_ref[...])
        o_ref[...] = _dot(hid.astype(BF16), wdb_ref[...])

    @pl.when(i >= nu_ref[0])
    def _():
        o_ref[...] = jnp.zeros_like(o_ref)


def _experts(blk_expert, n_used, buf, w_gate, w_up, w_down, layer):
    rows = buf.shape[0]
    n_blocks = rows // MOE_ROWS
    wspec_in = pl.BlockSpec((1, 1, D_MODEL, EXPERT_FF), lambda i, be, nu: (layer, be[i], 0, 0))
    wspec_out = pl.BlockSpec((1, 1, EXPERT_FF, D_MODEL), lambda i, be, nu: (layer, be[i], 0, 0))
    return pl.pallas_call(
        _expert_kernel,
        grid_spec=pltpu.PrefetchScalarGridSpec(
            num_scalar_prefetch=2,
            grid=(n_blocks,),
            in_specs=[pl.BlockSpec((MOE_ROWS, D_MODEL), lambda i, be, nu: (i, 0)), wspec_in, wspec_in, wspec_out],
            out_specs=pl.BlockSpec((MOE_ROWS, D_MODEL), lambda i, be, nu: (i, 0)),
            scratch_shapes=[pltpu.VMEM((D_MODEL, EXPERT_FF), BF16), pltpu.VMEM((D_MODEL, EXPERT_FF), BF16),
                            pltpu.VMEM((EXPERT_FF, D_MODEL), BF16)]),
        out_shape=jax.ShapeDtypeStruct((rows, D_MODEL), F32),
        compiler_params=_params("arbitrary"),
        name="moe_experts",
    )(blk_expert, n_used, buf, w_gate, w_up, w_down)


def _combine_kernel(x1_ref, y0_ref, y1_ref, route_ref, lnw_ref, lnb_ref, o_ref):
    route = route_ref[...]
    moe = y0_ref[...] * route[:, 2:3] + y1_ref[...] * route[:, 3:4]
    o_ref[...] = _layer_norm(DEEPNORM_ALPHA * x1_ref[...] + moe, lnw_ref[...], lnb_ref[...])


def _combine(x1, y0, y1, route, ln_w, ln_b):
    t = x1.shape[0]
    tm = min(ROW_TILE, t)
    row = lambda i: (i, 0)
    fixed = lambda i: (0, 0)
    big = pl.BlockSpec((tm, D_MODEL), row)
    return pl.pallas_call(
        _combine_kernel,
        grid=(t // tm,),
        in_specs=[big, big, big, pl.BlockSpec((tm, LANES), row),
                  pl.BlockSpec((1, D_MODEL), fixed), pl.BlockSpec((1, D_MODEL), fixed)],
        out_specs=big,
        out_shape=jax.ShapeDtypeStruct((t, D_MODEL), F32),
        compiler_params=_params("parallel"),
        name="moe_combine_ln",
    )(x1, y0, y1, route, ln_w, ln_b)


def _mix_heads(x2, l, b, seq, tables, w_in, nsa_cmp_pe, nsa_cmp_w1, nsa_cmp_w2, gdn_conv_w, gdn_a_log, gdn_dt_bias,
               gdn_norm_w, conf_dw_w, conf_dw_b, conf_ln_w, conf_ln_b):
    cos, sa, sb, oh, ov_t = tables
    t = b * seq
    n_c = seq // CMP_STRIDE
    n_gc = seq // GDN_CHUNK
    n_qb = seq // Q_BLOCK
    half = CMP_LEN // 2
    feat = CMP_STRIDE * HEAD_DIM
    q, kvc, ks, kw, vst, vwt, gqkv, gz, conf_in, small = _inproj(x2, _permute_w_in(w_in[l]), cos, sa, sb, oh, seq)

    chunks = kvc.reshape(b, seq, 2, NSA_KV_HEADS, HEAD_DIM).transpose(2, 0, 3, 1, 4)
    chunks = chunks.reshape(2, b * NSA_KV_HEADS, n_c, feat)
    pe = nsa_cmp_pe[l]
    kc, vct = _compress(chunks, pe[:, :half].reshape(2, 1, feat), pe[:, half:].reshape(2, 1, feat),
                        nsa_cmp_w1[l][:, :feat].astype(BF16), nsa_cmp_w1[l][:, feat:].astype(BF16),
                        nsa_cmp_w2[l].astype(BF16))
    gl = small[:, :3 * NSA_HEADS].reshape(b, n_qb, Q_BLOCK, NSA_KV_HEADS, NSA_GROUP, 3)
    gl = gl.transpose(0, 3, 1, 5, 4, 2).reshape(b, NSA_KV_HEADS, n_qb, 3, NSA_GROUP * Q_BLOCK)
    y_nsa = _nsa_attention(q, kc, vct, ks, vst, kw, vwt, gl, ov_t, b, seq)

    conv = _gdn_conv(gqkv.reshape(b, seq, 3 * GDN_WIDTH), gdn_conv_w[l])
    qkvh = conv.reshape(b, seq, 3, GDN_HEADS, HEAD_DIM).transpose(2, 0, 3, 1, 4)
    zh = gz.reshape(b, seq, GDN_HEADS, HEAD_DIM).transpose(0, 2, 1, 3)
    o_b = 3 * NSA_HEADS
    b_t = small[:, o_b:o_b + GDN_HEADS].reshape(b, seq, GDN_HEADS).transpose(0, 2, 1)
    a_t = small[:, o_b + GDN_HEADS:o_b + 2 * GDN_HEADS].reshape(b, seq, GDN_HEADS).transpose(0, 2, 1)
    y_gdn = _gdn(qkvh, zh, b_t[..., None], a_t[..., None], a_t.reshape(b, GDN_HEADS, n_gc, 1, GDN_CHUNK),
                 gdn_a_log[l].reshape(GDN_HEADS, 1, 1), gdn_dt_bias[l].reshape(GDN_HEADS, 1, 1),
                 gdn_norm_w[l].reshape(1, HEAD_DIM))
    y_gdn = y_gdn.transpose(0, 2, 1, 3).reshape(t, GDN_WIDTH)

    y_conf = _conformer(conf_in.reshape(b, seq, 2 * CONF_CH), conf_dw_w[l], conf_dw_b[l].reshape(1, CONF_CH),
                        conf_ln_w[l].reshape(1, CONF_CH), conf_ln_b[l].reshape(1, CONF_CH)).reshape(t, CONF_CH)
    return y_nsa, y_gdn, y_conf


def _moe_block(x1, x1b, logits, l, ln_w, ln_b, moe_w_gate, moe_w_up, moe_w_down):
    t = x1.shape[0]
    tk = 2 * t
    n_blocks = tk // MOE_ROWS + N_EXPERTS
    route, counts = _route(logits)
    eid = route[:, 0:2].astype(I32)
    pos = route[:, 4:6].astype(I32)
    cnt = counts[0, N_GROUPS:N_GROUPS + N_EXPERTS].astype(I32)
    padded = (cnt + MOE_ROWS - 1) // MOE_ROWS * MOE_ROWS
    pad_end = jnp.cumsum(padded)
    pad_start = pad_end - padded
    first_row = jnp.sum(jnp.where(eid[..., None] == jnp.arange(N_EXPERTS, dtype=I32), pad_start, 0), axis=-1)
    dest = first_row + pos
    src = jnp.zeros((n_blocks * MOE_ROWS,), I32).at[dest.reshape(tk)].set(jnp.arange(tk, dtype=I32) // 2)
    blk_row = jnp.arange(n_blocks, dtype=I32)[:, None] * MOE_ROWS
    blk_expert = jnp.minimum(jnp.sum((pad_end[None, :] <= blk_row).astype(I32), axis=1), N_EXPERTS - 1)
    n_used = (pad_end[-1:] // MOE_ROWS).astype(I32)
    y_buf = _experts(blk_expert, n_used, x1b[src], moe_w_gate, moe_w_up, moe_w_down, l)
    return _combine(x1, y_buf[dest[:, 0]], y_buf[dest[:, 1]], route, ln_w, ln_b)


def kernel(x, w_in, w_out, nsa_cmp_pe, nsa_cmp_w1, nsa_cmp_w2, gdn_conv_w, gdn_a_log, gdn_dt_bias, gdn_norm_w, conf_dw_w, conf_dw_b, conf_ln_w, conf_ln_b, ln1_w, ln1_b, ln2_w, ln2_b, moe_w_group, moe_b_group, moe_w_expert, moe_b_expert, moe_w_gate, moe_w_up, moe_w_down):
    b, seq, d = x.shape
    depth = w_in.shape[0]
    assert d == D_MODEL and seq % ROW_TILE == 0 and seq >= WINDOW + Q_BLOCK
    t = b * seq
    assert seq // SEL_BLOCK <= HEAD_DIM
    tables = _rope_tables(seq) + (_block_onehot(seq), _overlap_t(seq // CMP_STRIDE, seq // SEL_BLOCK))
    w_out_b = w_out.astype(BF16)
    n_pad = LANES - N_GROUPS - N_EXPERTS
    x2 = x.reshape(t, d)
    for l in range(depth):
        y_nsa, y_gdn, y_conf = _mix_heads(x2, l, b, seq, tables, w_in, nsa_cmp_pe, nsa_cmp_w1, nsa_cmp_w2, gdn_conv_w,
                                          gdn_a_log, gdn_dt_bias, gdn_norm_w, conf_dw_w, conf_dw_b, conf_ln_w, conf_ln_b)
        wr = jnp.concatenate([moe_w_group[l], moe_w_expert[l], jnp.zeros((d, n_pad), F32)], axis=1).astype(BF16)
        br = jnp.concatenate([moe_b_group[l], moe_b_expert[l], jnp.zeros((n_pad,), F32)]).reshape(1, LANES)
        x1, x1b, logits = _outproj(y_nsa, y_gdn, y_conf, x2, w_out_b[l], ln1_w[l].reshape(1, d),
                                   ln1_b[l].reshape(1, d), wr, br)
        x2 = _moe_block(x1, x1b, logits, l, ln2_w[l].reshape(1, d), ln2_b[l].reshape(1, d),
                        moe_w_gate, moe_w_up, moe_w_down)
    return x2.reshape(b, seq, d)
```

```python
import functools

import jax
import jax.numpy as jnp
import numpy as np
from jax import lax
from jax.experimental import pallas as pl
from jax.experimental.pallas import tpu as pltpu

F32 = jnp.float32
BF16 = jnp.bfloat16
I32 = jnp.int32

D_MODEL = 1024
DEPTH = 4
HEAD_DIM = 64
NSA_HEADS = 8
NSA_KV_HEADS = 2
NSA_GROUP = NSA_HEADS // NSA_KV_HEADS
CMP_LEN = 32
CMP_STRIDE = 16
CMP_HIDDEN = 128
SEL_BLOCK = 64
N_SEL = 16
WINDOW = 512
Q_BLOCK = 128
FORCE_BONUS = 1.0e4
ROPE_THETA = 500000.0
ROPE_DIM = HEAD_DIM // 4
GDN_HEADS = 4
GDN_CONV = 4
GDN_CHUNK = 64
CONF_CH = 256
CONF_WIDTH = 31
N_GROUPS = 4
EXPERTS_PER_GROUP = 8
N_EXPERTS = N_GROUPS * EXPERTS_PER_GROUP
EXPERT_FF = 512
NSA_WIDTH = NSA_HEADS * HEAD_DIM
KV_WIDTH = NSA_KV_HEADS * HEAD_DIM
GDN_WIDTH = GDN_HEADS * HEAD_DIM
DEEPNORM_ALPHA = (2.0 * DEPTH) ** 0.25
LN_EPS = 1e-5

LANES = 128
V7X_VMEM_BYTES = 64 * 1024 * 1024
VMEM_LIMIT = V7X_VMEM_BYTES * 3 // 4

Q_OFF = 0
KV_OFF = Q_OFF + NSA_WIDTH
GQKV_OFF = KV_OFF + 6 * KV_WIDTH
GZ_OFF = GQKV_OFF + 3 * GDN_WIDTH
CONF_OFF = GZ_OFF + GDN_WIDTH
SMALL_OFF = CONF_OFF + 2 * CONF_CH
IN_COLS = SMALL_OFF + LANES
N_SMALL = 3 * NSA_HEADS + 2 * GDN_HEADS

ROW_TILE = 512
MOE_ROWS = 512
GDN_BLOCK_CHUNKS = 4
GDN_BLOCK_BATCH = 4
SEL_KEYS = 512
NEG = -1e30


def _params(*sem):
    return pltpu.CompilerParams(dimension_semantics=sem, vmem_limit_bytes=VMEM_LIMIT)


def _dot(a, b):
    return jnp.dot(a, b, preferred_element_type=F32)


def _dot_nt(a, b):
    return lax.dot_general(a, b, (((1,), (1,)), ((), ())), preferred_element_type=F32)


def _bmm(a, b):
    return lax.dot_general(a.astype(BF16), b.astype(BF16), (((2,), (1,)), ((0,), (0,))),
                           preferred_element_type=F32)


def _bmm_nt(a, b):
    return lax.dot_general(a.astype(BF16), b.astype(BF16), (((2,), (2,)), ((0,), (0,))),
                           preferred_element_type=F32)


def _sigmoid(x):
    return 1.0 / (1.0 + jnp.exp(-x))


def _silu(x):
    return x * _sigmoid(x)


def _layer_norm(v, w, b):
    mu = jnp.mean(v, axis=-1, keepdims=True)
    d = v - mu
    var = jnp.mean(d * d, axis=-1, keepdims=True)
    return d * lax.rsqrt(var + LN_EPS) * w + b


def _inproj_kernel(x_ref, w_ref, cos_ref, sa_ref, sb_ref, oh_ref, q_ref, kvc_ref, ks_ref, kw_ref, vst_ref, vwt_ref,
                   gqkv_ref, gz_ref, conf_ref, small_ref):
    xb = x_ref[...].astype(BF16)
    cos = cos_ref[...]
    sa = sa_ref[...]
    sb = sb_ref[...]
    tm = xb.shape[0]

    def rope(h):
        return h * cos + pltpu.roll(h, LANES - ROPE_DIM // 2, 1) * sa + pltpu.roll(h, ROPE_DIM // 2, 1) * sb

    hq = _dot(xb, w_ref[:, Q_OFF:KV_OFF])
    scale = HEAD_DIM ** -0.5
    for j in range(NSA_WIDTH // LANES):
        q_ref[:, j * LANES:(j + 1) * LANES] = (rope(hq[:, j * LANES:(j + 1) * LANES]) * scale).astype(BF16)
    hkv = _dot(xb, w_ref[:, KV_OFF:GQKV_OFF])
    part = lambda j: hkv[:, j * LANES:(j + 1) * LANES]
    kvc_ref[:, 0:LANES] = rope(part(0)).astype(BF16)
    kvc_ref[:, LANES:2 * LANES] = part(1).astype(BF16)
    ksel = rope(part(2))
    first_half = lax.broadcasted_iota(I32, (tm, LANES), 1) < HEAD_DIM
    oh = oh_ref[...]
    ks_ref[:, 0:LANES] = jnp.where(first_half, ksel, oh).astype(BF16)
    ks_ref[:, LANES:2 * LANES] = jnp.where(first_half, pltpu.roll(ksel, HEAD_DIM, 1), oh).astype(BF16)
    kw_ref[...] = rope(part(4)).astype(BF16)
    vst_ref[0] = part(3).T.astype(BF16)
    vwt = part(5).T.astype(BF16)
    for c in range(tm // Q_BLOCK):
        vwt_ref[c] = vwt[:, c * Q_BLOCK:(c + 1) * Q_BLOCK]
    gqkv_ref[...] = _dot(xb, w_ref[:, GQKV_OFF:GZ_OFF])
    gz_ref[...] = _dot(xb, w_ref[:, GZ_OFF:CONF_OFF])
    conf_ref[...] = _dot(xb, w_ref[:, CONF_OFF:SMALL_OFF])
    small_ref[...] = _dot(xb, w_ref[:, SMALL_OFF:IN_COLS])


def _inproj(x2, w, cos, sa, sb, oh, seq):
    t = x2.shape[0]
    tm = ROW_TILE
    n_pos = seq // tm
    row = lambda i: (i, 0)
    pos = lambda i: (i % n_pos, 0)
    widths = (NSA_WIDTH, 2 * LANES, 2 * LANES, LANES, 3 * GDN_WIDTH, GDN_WIDTH, 2 * CONF_CH, LANES)
    dtypes = (BF16, BF16, BF16, BF16, F32, F32, F32, F32)
    flat_specs = [pl.BlockSpec((tm, wd), row) for wd in widths]
    flat_shapes = [jax.ShapeDtypeStruct((t, wd), dt) for wd, dt in zip(widths, dtypes)]
    n_q = tm // Q_BLOCK
    out_specs = flat_specs[:4] + [pl.BlockSpec((1, LANES, tm), lambda i: (i, 0, 0)),
                                  pl.BlockSpec((n_q, LANES, Q_BLOCK), lambda i: (i, 0, 0))] + flat_specs[4:]
    out_shape = flat_shapes[:4] + [jax.ShapeDtypeStruct((t // tm, LANES, tm), BF16),
                                   jax.ShapeDtypeStruct((t // Q_BLOCK, LANES, Q_BLOCK), BF16)] + flat_shapes[4:]
    return pl.pallas_call(
        _inproj_kernel,
        grid=(t // tm,),
        in_specs=[pl.BlockSpec((tm, D_MODEL), row),
                  pl.BlockSpec((D_MODEL, IN_COLS), lambda i: (0, 0)),
                  pl.BlockSpec((tm, LANES), pos), pl.BlockSpec((tm, LANES), pos), pl.BlockSpec((tm, LANES), pos),
                  pl.BlockSpec((tm, LANES), pos)],
        out_specs=out_specs,
        out_shape=out_shape,
        compiler_params=_params("parallel"),
        name="inproj",
    )(x2, w, cos, sa, sb, oh)


def _block_onehot(seq):
    blk = jnp.arange(seq)[:, None] // SEL_BLOCK
    lane = jnp.arange(LANES)[None, :]
    return jnp.where(lane - HEAD_DIM == blk, 1.0, 0.0).astype(F32)


def _rope_tables(seq):
    half = ROPE_DIM // 2
    inv = ROPE_THETA ** (-jnp.arange(0, ROPE_DIM, 2, dtype=F32) / ROPE_DIM)
    ang = jnp.arange(seq, dtype=F32)[:, None] * inv[None, :]
    c, s = jnp.cos(ang), jnp.sin(ang)
    ones = jnp.ones((seq, HEAD_DIM - ROPE_DIM), F32)
    zeros = jnp.zeros((seq, HEAD_DIM - ROPE_DIM), F32)
    zh = jnp.zeros((seq, half), F32)
    cos_h = jnp.concatenate([c, c, ones], axis=1)
    sa_h = jnp.concatenate([-s, zh, zeros], axis=1)
    sb_h = jnp.concatenate([zh, s, zeros], axis=1)
    rep = LANES // HEAD_DIM
    return jnp.tile(cos_h, (1, rep)), jnp.tile(sa_h, (1, rep)), jnp.tile(sb_h, (1, rep))


def _permute_w_in(w):
    o_q, o_kv = 0, NSA_WIDTH
    o_gate = o_kv + 6 * KV_WIDTH
    o_gqkv = o_gate + 3 * NSA_HEADS
    o_gz = o_gqkv + 3 * GDN_WIDTH
    o_b = o_gz + GDN_WIDTH
    o_a = o_b + GDN_HEADS
    o_conf = o_a + GDN_HEADS
    pad = jnp.zeros((w.shape[0], LANES - N_SMALL), w.dtype)
    return jnp.concatenate([w[:, o_q:o_gate], w[:, o_gqkv:o_b], w[:, o_conf:], w[:, o_gate:o_gqkv],
                            w[:, o_b:o_conf], pad], axis=1).astype(BF16)


def _compress_kernel(ch_ref, pet_ref, peb_ref, w1t_ref, w1b_ref, w2_ref, o_ref, ot_ref):
    ch = ch_ref[0, 0].astype(F32)
    top = _dot((ch + pet_ref[0]).astype(BF16), w1t_ref[0])
    bot = _dot((ch + peb_ref[0]).astype(BF16), w1b_ref[0])
    n = bot.shape[0]
    hid = top + pltpu.roll(bot, n - 1, 0)
    out = _dot(_silu(hid).astype(BF16), w2_ref[0])
    o_ref[0, 0] = out.astype(o_ref.dtype)
    ot_ref[0, 0] = jnp.concatenate([out, jnp.zeros_like(out)], axis=1).T[:HEAD_DIM].astype(ot_ref.dtype)


def _compress(chunks, pe_top, pe_bot, w1_top, w1_bot, w2):
    _, bh, n_chunk, feat = chunks.shape
    per = lambda r, i: (r, 0, 0)
    return pl.pallas_call(
        _compress_kernel,
        grid=(2, bh),
        in_specs=[pl.BlockSpec((1, 1, n_chunk, feat), lambda r, i: (r, i, 0, 0)),
                  pl.BlockSpec((1, 1, feat), per), pl.BlockSpec((1, 1, feat), per),
                  pl.BlockSpec((1, feat, CMP_HIDDEN), per), pl.BlockSpec((1, feat, CMP_HIDDEN), per),
                  pl.BlockSpec((1, CMP_HIDDEN, HEAD_DIM), per)],
        out_specs=[pl.BlockSpec((1, 1, n_chunk, HEAD_DIM), lambda r, i: (r, i, 0, 0)),
                   pl.BlockSpec((1, 1, HEAD_DIM, n_chunk), lambda r, i: (r, i, 0, 0))],
        out_shape=[jax.ShapeDtypeStruct((2, bh, n_chunk, HEAD_DIM), BF16),
                   jax.ShapeDtypeStruct((2, bh, HEAD_DIM, n_chunk), BF16)],
        compiler_params=_params("parallel", "parallel"),
        name="nsa_compress",
    )(chunks, pe_top, pe_bot, w1_top, w1_bot, w2)


def _nsa_kernel(q_ref, kc_ref, vct_ref, ks_ref, vst_ref, kw_ref, vwt_ref, gl_ref, ovt_ref, o_ref, *, seq, n_sel):
    g, qn = NSA_GROUP, Q_BLOCK
    lanes = g * qn
    n_c = seq // CMP_STRIDE
    head = pl.program_id(1)
    qb = pl.program_id(2)
    s0 = qb * qn
    tile = lambda a: jnp.concatenate([a] * g, axis=1)
    qt4 = q_ref[...].astype(F32).T
    q_t = jnp.concatenate([qt4[i * HEAD_DIM:(i + 1) * HEAD_DIM] for i in range(g)], axis=1).astype(BF16)
    t_row = s0 + (lax.broadcasted_iota(I32, (1, lanes), 1) & (qn - 1))
    ones_rows = jnp.ones((16, ROW_TILE), BF16)

    sc = _dot(kc_ref[0, 0], q_t)
    cend = lax.broadcasted_iota(I32, (n_c, lanes), 0) * CMP_STRIDE + (CMP_LEN - 1)
    sc = jnp.where(cend <= t_row, sc, -jnp.inf)
    m = jnp.max(sc, axis=0, keepdims=True)
    m = jnp.where(m == -jnp.inf, 0.0, m)
    e = jnp.exp(sc - m)
    pc = e * (1.0 / jnp.maximum(jnp.sum(e, axis=0, keepdims=True), 1e-30))
    o_cmp = _dot(vct_ref[0, 0], pc.astype(BF16))

    pcs = pc[:, 0:qn]
    for i in range(1, g):
        pcs = pcs + pc[:, i * qn:(i + 1) * qn]
    imp = jnp.dot(ovt_ref[...], pcs, preferred_element_type=F32, precision=lax.Precision.HIGHEST)
    n_blk = imp.shape[0]
    blk = lax.broadcasted_iota(I32, (n_blk, qn), 0)
    cur = (s0 + lax.broadcasted_iota(I32, (n_blk, qn), 1)) // SEL_BLOCK
    forced = (blk == 0) | (blk == cur) | (blk == cur - 1)
    causal = blk <= cur
    imp = jnp.where(causal, imp + jnp.where(forced, FORCE_BONUS, 0.0), -jnp.inf)
    rank = jnp.zeros((n_blk, qn), F32)
    for i in range(n_blk):
        vi = imp[i:i + 1, :]
        ahead = (vi > imp) | ((vi == imp) & (blk > i))
        rank = rank + jnp.where(ahead, 1.0, 0.0)
    sel_bias = jnp.where((rank < n_sel) & causal, 0.0, NEG).astype(BF16)
    q_aug = jnp.concatenate([q_t, tile(sel_bias)], axis=0)

    zeros = jnp.zeros_like(q_t)
    q_win = jnp.where(head == 0, jnp.concatenate([q_t, zeros], axis=0), jnp.concatenate([zeros, q_t], axis=0))
    n_wb = WINDOW // qn + 1
    sw = []
    for kb in range(n_wb):
        bi = qb - (n_wb - 1) + kb
        k0 = pl.multiple_of(jnp.maximum(bi, 0) * qn, qn)
        pos = bi * qn + lax.broadcasted_iota(I32, (qn, qn), 0)
        tq = s0 + lax.broadcasted_iota(I32, (qn, qn), 1)
        ok = (pos >= 0) & (pos <= tq) & (pos > tq - WINDOW)
        sw.append(_dot(kw_ref[pl.ds(k0, qn), :], q_win) + tile(jnp.where(ok, 0.0, NEG)))
    m_w = jnp.max(sw[0], axis=0, keepdims=True)
    for kb in range(1, n_wb):
        m_w = jnp.maximum(m_w, jnp.max(sw[kb], axis=0, keepdims=True))
    acc_w = jnp.zeros((HEAD_DIM + 16, lanes), F32)
    for kb in range(n_wb):
        vt = jnp.concatenate([vwt_ref[0, jnp.maximum(qb - (n_wb - 1) + kb, 0)], ones_rows[:, 0:qn]], axis=0)
        acc_w = acc_w + _dot(vt, jnp.exp(sw[kb] - m_w).astype(BF16))
    o_win = acc_w[0:HEAD_DIM] * (1.0 / acc_w[HEAD_DIM:HEAD_DIM + 1])

    kstep = ROW_TILE
    c_last = s0 // kstep

    def sel_scores(c):
        k0 = pl.multiple_of(c * kstep, kstep)
        return _dot(ks_ref[pl.ds(k0, kstep), :], q_aug)

    def sel_values(c):
        return jnp.concatenate([vst_ref[0, c], ones_rows], axis=0)

    def sel_update(c, m_i, acc, s):
        m_new = jnp.maximum(m_i, jnp.max(s, axis=0, keepdims=True))
        p = jnp.exp(s - m_new).astype(BF16)
        return m_new, jnp.exp(m_i - m_new) * acc + _dot(sel_values(c), p)

    def sel_step(c, carry):
        m_i, acc, s = carry
        s_next = sel_scores(c + 1)
        return sel_update(c, m_i, acc, s) + (s_next,)

    init = (jnp.full((1, lanes), NEG, F32), jnp.zeros((HEAD_DIM + 16, lanes), F32), sel_scores(0))
    m_s, acc_s, s_last = lax.fori_loop(0, c_last, sel_step, init)
    key = c_last * kstep + lax.broadcasted_iota(I32, (kstep, qn), 0)
    future = jnp.where(key <= s0 + lax.broadcasted_iota(I32, (kstep, qn), 1), 0.0, NEG)
    _, acc_s = sel_update(c_last, m_s, acc_s, s_last + tile(future))
    o_sel = acc_s[0:HEAD_DIM] * (1.0 / acc_s[HEAD_DIM:HEAD_DIM + 1])

    gate = _sigmoid(gl_ref[0, 0, 0])
    o = o_cmp * gate[0:1] + o_sel * gate[1:2] + o_win * gate[2:3]
    o4 = jnp.concatenate([o[:, i * qn:(i + 1) * qn] for i in range(g)], axis=0)
    o_ref[...] = o4.T.astype(o_ref.dtype)


def _overlap_t(n_c, n_sb):
    c0 = jnp.arange(n_c)[None, :] * CMP_STRIDE
    b0 = jnp.arange(HEAD_DIM)[:, None] * SEL_BLOCK
    ov = jnp.minimum(c0 + CMP_LEN, b0 + SEL_BLOCK) - jnp.maximum(c0, b0)
    ov = (jnp.maximum(ov, 0) / CMP_STRIDE).astype(F32)
    return jnp.where(jnp.arange(HEAD_DIM)[:, None] < n_sb, ov, 0.0)


def _nsa_attention(q, kc, vct, ks, vst, kw, vwt, gl, ov_t, b, seq):
    n_c = seq // CMP_STRIDE
    n_sb = seq // SEL_BLOCK
    n_qb = seq // Q_BLOCK
    n_sel = min(N_SEL, n_sb)
    t = b * seq
    width = NSA_GROUP * HEAD_DIM
    vst4 = vst.reshape(b, seq // ROW_TILE, LANES, ROW_TILE)
    vwt4 = vwt.reshape(b, n_qb, LANES, Q_BLOCK)
    return pl.pallas_call(
        functools.partial(_nsa_kernel, seq=seq, n_sel=n_sel),
        grid=(b, NSA_KV_HEADS, n_qb),
        in_specs=[pl.BlockSpec((Q_BLOCK, width), lambda bi, h, i: (bi * n_qb + i, h)),
                  pl.BlockSpec((1, 1, n_c, HEAD_DIM), lambda bi, h, i: (0, bi * NSA_KV_HEADS + h, 0, 0)),
                  pl.BlockSpec((1, 1, HEAD_DIM, n_c), lambda bi, h, i: (1, bi * NSA_KV_HEADS + h, 0, 0)),
                  pl.BlockSpec((seq, LANES), lambda bi, h, i: (bi, h)),
                  pl.BlockSpec((1, seq // ROW_TILE, HEAD_DIM, ROW_TILE), lambda bi, h, i: (bi, 0, h, 0)),
                  pl.BlockSpec((seq, LANES), lambda bi, h, i: (bi, 0)),
                  pl.BlockSpec((1, n_qb, HEAD_DIM, Q_BLOCK), lambda bi, h, i: (bi, 0, h, 0)),
                  pl.BlockSpec((1, 1, 1, 3, NSA_GROUP * Q_BLOCK), lambda bi, h, i: (bi, h, i, 0, 0)),
                  pl.BlockSpec((HEAD_DIM, n_c), lambda bi, h, i: (0, 0))],
        out_specs=pl.BlockSpec((Q_BLOCK, width), lambda bi, h, i: (bi * n_qb + i, h)),
        out_shape=jax.ShapeDtypeStruct((t, NSA_WIDTH), BF16),
        compiler_params=_params("parallel", "parallel", "arbitrary"),
        name="nsa_attention",
    )(q, kc, vct, ks, vst4, kw, vwt4, gl, ov_t)


def _gdn_conv_kernel(cur_ref, prev_ref, w_ref, o_ref, ext_ref):
    ts = cur_ref.shape[1]
    halo = prev_ref.shape[1]
    first = pl.program_id(1) == 0
    ext_ref[0:halo, :] = jnp.where(first, 0.0, prev_ref[0])
    ext_ref[halo:halo + ts, :] = cur_ref[0]
    acc = cur_ref[0] * w_ref[GDN_CONV - 1:GDN_CONV, :]
    for k in range(GDN_CONV - 1):
        acc = acc + ext_ref[pl.ds(halo - (GDN_CONV - 1) + k, ts), :] * w_ref[k:k + 1, :]
    o_ref[0] = _silu(acc)


def _gdn_conv(x3, w):
    b, seq, ch = x3.shape
    ts = min(ROW_TILE, seq)
    halo = 8
    per_tile = ts // halo
    return pl.pallas_call(
        _gdn_conv_kernel,
        grid=(b, seq // ts),
        in_specs=[pl.BlockSpec((1, ts, ch), lambda bi, i: (bi, i, 0)),
                  pl.BlockSpec((1, halo, ch), lambda bi, i: (bi, jnp.maximum(i * per_tile - 1, 0), 0)),
                  pl.BlockSpec((GDN_CONV, ch), lambda bi, i: (0, 0))],
        out_specs=pl.BlockSpec((1, ts, ch), lambda bi, i: (bi, i, 0)),
        out_shape=jax.ShapeDtypeStruct((b, seq, ch), F32),
        scratch_shapes=[pltpu.VMEM((ts + halo, ch), F32)],
        compiler_params=_params("parallel", "parallel"),
        name="gdn_conv",
    )(x3, x3, w)


def _l2norm(v):
    return v * lax.rsqrt(jnp.sum(v * v, axis=-1, keepdims=True) + 1e-6)


def _softplus(v):
    return jnp.maximum(v, 0.0) + jnp.log1p(jnp.exp(-jnp.abs(v)))


def _gdn_kernel(q_ref, k_ref, v_ref, z_ref, b_ref, ac_ref, ar_ref, alog_ref, dtb_ref, nw_ref, o_ref, state_ref,
                *, chunks):
    c_len = GDN_CHUNK

    @pl.when(pl.program_id(1) == 0)
    def _():
        state_ref[...] = jnp.zeros_like(state_ref)

    ri = lax.broadcasted_iota(I32, (c_len, c_len), 0)
    ci = lax.broadcasted_iota(I32, (c_len, c_len), 1)
    lower = (ri >= ci)[None]
    strict = (ri > ci)[None]
    upper = (ri <= ci)[None]
    eye = jnp.where(ri == ci, 1.0, 0.0)[None]
    nb, nh = z_ref.shape[0], z_ref.shape[1]
    n = nb * nh
    rate = jnp.concatenate([-jnp.exp(alog_ref[...])] * nb, axis=0)
    dtb = jnp.concatenate([dtb_ref[...]] * nb, axis=0)
    nw = nw_ref[...]

    def chunk(c, carry):
        r0 = pl.multiple_of(c * c_len, c_len)
        rows = pl.ds(r0, c_len)
        q = _l2norm(q_ref[0, :, :, rows, :].reshape(n, c_len, HEAD_DIM)) * HEAD_DIM ** -0.5
        k = _l2norm(k_ref[0, :, :, rows, :].reshape(n, c_len, HEAD_DIM))
        v = v_ref[0, :, :, rows, :].reshape(n, c_len, HEAD_DIM)
        beta = _sigmoid(b_ref[:, :, rows, :].reshape(n, c_len, 1))
        g_col = rate * _softplus(ac_ref[:, :, rows, :].reshape(n, c_len, 1) + dtb)
        g_row = rate * _softplus(ar_ref[:, :, c].reshape(n, 1, c_len) + dtb)
        gc_col = jnp.sum(jnp.where(lower, g_row, 0.0), axis=2, keepdims=True)
        gc_row = jnp.sum(jnp.where(upper, g_col, 0.0), axis=1, keepdims=True)
        decay = jnp.exp(jnp.where(lower, gc_col - gc_row, -jnp.inf))
        kb = k * beta
        a = jnp.where(strict, _bmm_nt(kb, k) * decay, 0.0)
        inv = eye - a
        pw = a
        for _ in range(int(np.log2(c_len)) - 1):
            pw = _bmm(pw, pw)
            inv = inv + _bmm(inv, pw)
        e_col = jnp.exp(gc_col)
        u = _bmm(inv, v * beta)
        w = _bmm(inv, kb * e_col)
        qk = _bmm_nt(q, k) * decay
        g_last = gc_col[:, c_len - 1:c_len, :]
        k_dec = k * jnp.exp(g_last - gc_col)
        state = state_ref[...]
        v_new = u - _bmm(w, state)
        o = _bmm(q * e_col, state) + _bmm(qk, v_new)
        state_ref[...] = state * jnp.exp(g_last) + _bmm(jnp.swapaxes(k_dec, 1, 2), v_new)
        o = o * lax.rsqrt(jnp.mean(o * o, axis=-1, keepdims=True) + 1e-6) * nw
        o = o * _silu(z_ref[:, :, rows, :].reshape(n, c_len, HEAD_DIM))
        o_ref[:, :, rows, :] = o.reshape(nb, nh, c_len, HEAD_DIM).astype(o_ref.dtype)
        return carry

    lax.fori_loop(0, chunks, chunk, 0)


def _gdn(qkvh, zh, b_col, a_col, a_row, a_log, dt_bias, norm_w):
    _, b, h, seq, dh = qkvh.shape
    chunks = GDN_BLOCK_CHUNKS
    rows = chunks * GDN_CHUNK
    nb = GDN_BLOCK_BATCH if b % GDN_BLOCK_BATCH == 0 else 1

    def qkv_spec(r):
        return pl.BlockSpec((1, nb, h, rows, dh), lambda bi, i: (r, bi, 0, i, 0))

    col = pl.BlockSpec((nb, h, rows, 1), lambda bi, i: (bi, 0, i, 0))
    head = pl.BlockSpec((h, 1, 1), lambda bi, i: (0, 0, 0))
    return pl.pallas_call(
        functools.partial(_gdn_kernel, chunks=chunks),
        grid=(b // nb, seq // rows),
        in_specs=[qkv_spec(0), qkv_spec(1), qkv_spec(2),
                  pl.BlockSpec((nb, h, rows, dh), lambda bi, i: (bi, 0, i, 0)),
                  col, col,
                  pl.BlockSpec((nb, h, chunks, 1, GDN_CHUNK), lambda bi, i: (bi, 0, i, 0, 0)),
                  head, head,
                  pl.BlockSpec((1, dh), lambda bi, i: (0, 0))],
        out_specs=pl.BlockSpec((nb, h, rows, dh), lambda bi, i: (bi, 0, i, 0)),
        out_shape=jax.ShapeDtypeStruct((b, h, seq, dh), BF16),
        scratch_shapes=[pltpu.VMEM((nb * h, dh, dh), F32)],
        compiler_params=_params("parallel", "arbitrary"),
        name="gdn_delta_rule",
    )(qkvh, qkvh, qkvh, zh, b_col, a_col, a_row, a_log, dt_bias, norm_w)


def _conf_kernel(cur_ref, prev_ref, w_ref, b_ref, lnw_ref, lnb_ref, o_ref, ext_ref):
    ts = cur_ref.shape[1]
    halo = prev_ref.shape[1]
    first = pl.program_id(1) == 0

    def glu(blk):
        return blk[:, :CONF_CH] * _sigmoid(blk[:, CONF_CH:])

    ext_ref[0:halo, :] = jnp.where(first, 0.0, glu(prev_ref[0]))
    ext_ref[halo:halo + ts, :] = glu(cur_ref[0])
    acc = jnp.zeros((ts, CONF_CH), F32) + b_ref[...]
    for k in range(CONF_WIDTH):
        acc = acc + ext_ref[pl.ds(halo - (CONF_WIDTH - 1) + k, ts), :] * w_ref[k:k + 1, :]
    o_ref[0] = _silu(_layer_norm(acc, lnw_ref[...], lnb_ref[...])).astype(o_ref.dtype)


def _conformer(x3, w, bias, ln_w, ln_b):
    b, seq, ch = x3.shape
    ts = min(ROW_TILE, seq)
    halo = 32
    per_tile = ts // halo
    vec = pl.BlockSpec((1, CONF_CH), lambda bi, i: (0, 0))
    return pl.pallas_call(
        _conf_kernel,
        grid=(b, seq // ts),
        in_specs=[pl.BlockSpec((1, ts, ch), lambda bi, i: (bi, i, 0)),
                  pl.BlockSpec((1, halo, ch), lambda bi, i: (bi, jnp.maximum(i * per_tile - 1, 0), 0)),
                  pl.BlockSpec((CONF_WIDTH, CONF_CH), lambda bi, i: (0, 0)), vec, vec, vec],
        out_specs=pl.BlockSpec((1, ts, CONF_CH), lambda bi, i: (bi, i, 0)),
        out_shape=jax.ShapeDtypeStruct((b, seq, CONF_CH), BF16),
        scratch_shapes=[pltpu.VMEM((ts + halo, CONF_CH), F32)],
        compiler_params=_params("parallel", "parallel"),
        name="conformer_conv",
    )(x3, x3, w, bias, ln_w, ln_b)


def _outproj_kernel(yn_ref, yg_ref, yc_ref, x_ref, w_ref, lnw_ref, lnb_ref, wr_ref, br_ref, x1_ref, x1b_ref, lg_ref):
    mix = _dot(yn_ref[...], w_ref[0:NSA_WIDTH, :])
    mix = mix + _dot(yg_ref[...], w_ref[NSA_WIDTH:NSA_WIDTH + GDN_WIDTH, :])
    mix = mix + _dot(yc_ref[...], w_ref[NSA_WIDTH + GDN_WIDTH:, :])
    x1 = _layer_norm(DEEPNORM_ALPHA * x_ref[...] + mix, lnw_ref[...], lnb_ref[...])
    x1_ref[...] = x1
    xb = x1.astype(BF16)
    x1b_ref[...] = xb
    lg_ref[...] = _dot(xb, wr_ref[...]) + br_ref[...]


def _outproj(y_nsa, y_gdn, y_conf, x2, w, ln_w, ln_b, wr, br):
    t = x2.shape[0]
    tm = min(ROW_TILE, t)
    row = lambda i: (i, 0)
    fixed = lambda i: (0, 0)
    return pl.pallas_call(
        _outproj_kernel,
        grid=(t // tm,),
        in_specs=[pl.BlockSpec((tm, NSA_WIDTH), row), pl.BlockSpec((tm, GDN_WIDTH), row),
                  pl.BlockSpec((tm, CONF_CH), row), pl.BlockSpec((tm, D_MODEL), row),
                  pl.BlockSpec((D_MODEL, D_MODEL), fixed), pl.BlockSpec((1, D_MODEL), fixed),
                  pl.BlockSpec((1, D_MODEL), fixed), pl.BlockSpec((D_MODEL, LANES), fixed),
                  pl.BlockSpec((1, LANES), fixed)],
        out_specs=[pl.BlockSpec((tm, D_MODEL), row), pl.BlockSpec((tm, D_MODEL), row), pl.BlockSpec((tm, LANES), row)],
        out_shape=[jax.ShapeDtypeStruct((t, D_MODEL), F32), jax.ShapeDtypeStruct((t, D_MODEL), BF16),
                   jax.ShapeDtypeStruct((t, LANES), F32)],
        compiler_params=_params("parallel"),
        name="outproj_ln_router",
    )(y_nsa, y_gdn, y_conf, x2, w, ln_w, ln_b, wr, br)


def _route_kernel(lg_ref, route_ref, counts_ref, carry_ref):
    tm = lg_ref.shape[0]

    @pl.when(pl.program_id(0) == 0)
    def _():
        carry_ref[...] = jnp.zeros_like(carry_ref)

    lg = lg_ref[...]
    lane = lax.broadcasted_iota(I32, (tm, LANES), 1)
    lane_f = lane.astype(F32)

    def top1(vals):
        best = jnp.max(vals, axis=1, keepdims=True)
        idx = jnp.min(jnp.where(vals == best, lane_f, float(LANES)), axis=1, keepdims=True)
        return best, idx

    gl = jnp.where(lane < N_GROUPS, lg, -jnp.inf)
    g_best, g_idx = top1(gl)
    g_w = 1.0 / jnp.sum(jnp.exp(gl - g_best), axis=1, keepdims=True)
    lo = N_GROUPS + g_idx * EXPERTS_PER_GROUP
    el = jnp.where((lane_f >= lo) & (lane_f < lo + EXPERTS_PER_GROUP), lg, -jnp.inf)
    e1, i1 = top1(el)
    e2, i2 = top1(jnp.where(lane_f == i1, -jnp.inf, el))
    r = jnp.exp(e2 - e1)
    w1 = g_w / (1.0 + r)
    w2 = g_w * r / (1.0 + r)
    hit1 = lane_f == i1
    hit2 = lane_f == i2
    onehot = jnp.where(hit1 | hit2, 1.0, 0.0)
    ri = lax.broadcasted_iota(I32, (tm, tm), 0)
    ci = lax.broadcasted_iota(I32, (tm, tm), 1)
    before = jnp.where(ri > ci, 1.0, 0.0).astype(BF16)
    seen = carry_ref[...] + _dot(before, onehot.astype(BF16))
    p1 = jnp.sum(jnp.where(hit1, seen, 0.0), axis=1, keepdims=True)
    p2 = jnp.sum(jnp.where(hit2, seen, 0.0), axis=1, keepdims=True)
    total = carry_ref[...] + jnp.sum(onehot, axis=0, keepdims=True)
    carry_ref[...] = total
    counts_ref[...] = total
    cols = (i1 - N_GROUPS, i2 - N_GROUPS, w1, w2, p1, p2)
    out = jnp.zeros((tm, LANES), F32)
    for j, cval in enumerate(cols):
        out = jnp.where(lane == j, cval, out)
    route_ref[...] = out


def _route(logits):
    t = logits.shape[0]
    tm = min(ROW_TILE, t)
    return pl.pallas_call(
        _route_kernel,
        grid=(t // tm,),
        in_specs=[pl.BlockSpec((tm, LANES), lambda i: (i, 0))],
        out_specs=[pl.BlockSpec((tm, LANES), lambda i: (i, 0)), pl.BlockSpec((1, LANES), lambda i: (0, 0))],
        out_shape=[jax.ShapeDtypeStruct((t, LANES), F32), jax.ShapeDtypeStruct((1, LANES), F32)],
        scratch_shapes=[pltpu.VMEM((1, LANES), F32)],
        compiler_params=_params("arbitrary"),
        name="moe_route",
    )(logits)


def _expert_kernel(be_ref, nu_ref, x_ref, wg_ref, wu_ref, wd_ref, o_ref, wgb_ref, wub_ref, wdb_ref):
    i = pl.program_id(0)
    changed = (i == 0) | (be_ref[i] != be_ref[jnp.maximum(i - 1, 0)])

    @pl.when(changed)
    def _():
        wgb_ref[...] = wg_ref[0, 0].astype(BF16)
        wub_ref[...] = wu_ref[0, 0].astype(BF16)
        wdb_ref[...] = wd_ref[0, 0].astype(BF16)

    @pl.when(i < nu_ref[0])
    def _():
        xb = x_ref[...]
        hid = _silu(_dot(xb, wgb_ref[...])) * _dot(xb, wub_ref[...])
        o_ref[...] = _dot(hid.astype(BF16), wdb_ref[...])

    @pl.when(i >= nu_ref[0])
    def _():
        o_ref[...] = jnp.zeros_like(o_ref)


def _experts(blk_expert, n_used, buf, w_gate, w_up, w_down, layer):
    rows = buf.shape[0]
    n_blocks = rows // MOE_ROWS
    wspec_in = pl.BlockSpec((1, 1, D_MODEL, EXPERT_FF), lambda i, be, nu: (layer, be[i], 0, 0))
    wspec_out = pl.BlockSpec((1, 1, EXPERT_FF, D_MODEL), lambda i, be, nu: (layer, be[i], 0, 0))
    return pl.pallas_call(
        _expert_kernel,
        grid_spec=pltpu.PrefetchScalarGridSpec(
            num_scalar_prefetch=2,
            grid=(n_blocks,),
            in_specs=[pl.BlockSpec((MOE_ROWS, D_MODEL), lambda i, be, nu: (i, 0)), wspec_in, wspec_in, wspec_out],
            out_specs=pl.BlockSpec((MOE_ROWS, D_MODEL), lambda i, be, nu: (i, 0)),
            scratch_shapes=[pltpu.VMEM((D_MODEL, EXPERT_FF), BF16), pltpu.VMEM((D_MODEL, EXPERT_FF), BF16),
                            pltpu.VMEM((EXPERT_FF, D_MODEL), BF16)]),
        out_shape=jax.ShapeDtypeStruct((rows, D_MODEL), F32),
        compiler_params=_params("arbitrary"),
        name="moe_experts",
    )(blk_expert, n_used, buf, w_gate, w_up, w_down)


def _combine_kernel(x1_ref, y0_ref, y1_ref, route_ref, lnw_ref, lnb_ref, o_ref):
    route = route_ref[...]
    moe = y0_ref[...] * route[:, 2:3] + y1_ref[...] * route[:, 3:4]
    o_ref[...] = _layer_norm(DEEPNORM_ALPHA * x1_ref[...] + moe, lnw_ref[...], lnb_ref[...])


def _combine(x1, y0, y1, route, ln_w, ln_b):
    t = x1.shape[0]
    tm = min(ROW_TILE, t)
    row = lambda i: (i, 0)
    fixed = lambda i: (0, 0)
    big = pl.BlockSpec((tm, D_MODEL), row)
    return pl.pallas_call(
        _combine_kernel,
        grid=(t // tm,),
        in_specs=[big, big, big, pl.BlockSpec((tm, LANES), row),
                  pl.BlockSpec((1, D_MODEL), fixed), pl.BlockSpec((1, D_MODEL), fixed)],
        out_specs=big,
        out_shape=jax.ShapeDtypeStruct((t, D_MODEL), F32),
        compiler_params=_params("parallel"),
        name="moe_combine_ln",
    )(x1, y0, y1, route, ln_w, ln_b)


def _mix_heads(x2, l, b, seq, tables, w_in, nsa_cmp_pe, nsa_cmp_w1, nsa_cmp_w2, gdn_conv_w, gdn_a_log, gdn_dt_bias,
               gdn_norm_w, conf_dw_w, conf_dw_b, conf_ln_w, conf_ln_b):
    cos, sa, sb, oh, ov_t = tables
    t = b * seq
    n_c = seq // CMP_STRIDE
    n_gc = seq // GDN_CHUNK
    n_qb = seq // Q_BLOCK
    half = CMP_LEN // 2
    feat = CMP_STRIDE * HEAD_DIM
    q, kvc, ks, kw, vst, vwt, gqkv, gz, conf_in, small = _inproj(x2, _permute_w_in(w_in[l]), cos, sa, sb, oh, seq)

    chunks = kvc.reshape(b, seq, 2, NSA_KV_HEADS, HEAD_DIM).transpose(2, 0, 3, 1, 4)
    chunks = chunks.reshape(2, b * NSA_KV_HEADS, n_c, feat)
    pe = nsa_cmp_pe[l]
    kc, vct = _compress(chunks, pe[:, :half].reshape(2, 1, feat), pe[:, half:].reshape(2, 1, feat),
                        nsa_cmp_w1[l][:, :feat].astype(BF16), nsa_cmp_w1[l][:, feat:].astype(BF16),
                        nsa_cmp_w2[l].astype(BF16))
    gl = small[:, :3 * NSA_HEADS].reshape(b, n_qb, Q_BLOCK, NSA_KV_HEADS, NSA_GROUP, 3)
    gl = gl.transpose(0, 3, 1, 5, 4, 2).reshape(b, NSA_KV_HEADS, n_qb, 3, NSA_GROUP * Q_BLOCK)
    y_nsa = _nsa_attention(q, kc, vct, ks, vst, kw, vwt, gl, ov_t, b, seq)

    conv = _gdn_conv(gqkv.reshape(b, seq, 3 * GDN_WIDTH), gdn_conv_w[l])
    qkvh = conv.reshape(b, seq, 3, GDN_HEADS, HEAD_DIM).transpose(2, 0, 3, 1, 4)
    zh = gz.reshape(b, seq, GDN_HEADS, HEAD_DIM).transpose(0, 2, 1, 3)
    o_b = 3 * NSA_HEADS
    b_t = small[:, o_b:o_b + GDN_HEADS].reshape(b, seq, GDN_HEADS).transpose(0, 2, 1)
    a_t = small[:, o_b + GDN_HEADS:o_b + 2 * GDN_HEADS].reshape(b, seq, GDN_HEADS).transpose(0, 2, 1)
    y_gdn = _gdn(qkvh, zh, b_t[..., None], a_t[..., None], a_t.reshape(b, GDN_HEADS, n_gc, 1, GDN_CHUNK),
                 gdn_a_log[l].reshape(GDN_HEADS, 1, 1), gdn_dt_bias[l].reshape(GDN_HEADS, 1, 1),
                 gdn_norm_w[l].reshape(1, HEAD_DIM))
    y_gdn = y_gdn.transpose(0, 2, 1, 3).reshape(t, GDN_WIDTH)

    y_conf = _conformer(conf_in.reshape(b, seq, 2 * CONF_CH), conf_dw_w[l], conf_dw_b[l].reshape(1, CONF_CH),
                        conf_ln_w[l].reshape(1, CONF_CH), conf_ln_b[l].reshape(1, CONF_CH)).reshape(t, CONF_CH)
    return y_nsa, y_gdn, y_conf


def _moe_block(x1, x1b, logits, l, ln_w, ln_b, moe_w_gate, moe_w_up, moe_w_down):
    t = x1.shape[0]
    tk = 2 * t
    n_blocks = tk // MOE_ROWS + N_EXPERTS
    route, counts = _route(logits)
    eid = route[:, 0:2].astype(I32)
    pos = route[:, 4:6].astype(I32)
    cnt = counts[0, N_GROUPS:N_GROUPS + N_EXPERTS].astype(I32)
    padded = (cnt + MOE_ROWS - 1) // MOE_ROWS * MOE_ROWS
    pad_end = jnp.cumsum(padded)
    pad_start = pad_end - padded
    first_row = jnp.sum(jnp.where(eid[..., None] == jnp.arange(N_EXPERTS, dtype=I32), pad_start, 0), axis=-1)
    dest = first_row + pos
    spare = jnp.arange(n_blocks * MOE_ROWS, dtype=I32) % t
    src = spare.at[dest.reshape(tk)].set(jnp.arange(tk, dtype=I32) // 2)
    blk_row = jnp.arange(n_blocks, dtype=I32)[:, None] * MOE_ROWS
    blk_expert = jnp.minimum(jnp.sum((pad_end[None, :] <= blk_row).astype(I32), axis=1), N_EXPERTS - 1)
    n_used = (pad_end[-1:] // MOE_ROWS).astype(I32)
    y_buf = _experts(blk_expert, n_used, x1b[src], moe_w_gate, moe_w_up, moe_w_down, l)
    return _combine(x1, y_buf[dest[:, 0]], y_buf[dest[:, 1]], route, ln_w, ln_b)


def kernel(x, w_in, w_out, nsa_cmp_pe, nsa_cmp_w1, nsa_cmp_w2, gdn_conv_w, gdn_a_log, gdn_dt_bias, gdn_norm_w, conf_dw_w, conf_dw_b, conf_ln_w, conf_ln_b, ln1_w, ln1_b, ln2_w, ln2_b, moe_w_group, moe_b_group, moe_w_expert, moe_b_expert, moe_w_gate, moe_w_up, moe_w_down):
    b, seq, d = x.shape
    depth = w_in.shape[0]
    assert d == D_MODEL and seq % ROW_TILE == 0 and seq >= WINDOW + Q_BLOCK
    t = b * seq
    assert seq // SEL_BLOCK <= HEAD_DIM
    tables = _rope_tables(seq) + (_block_onehot(seq), _overlap_t(seq // CMP_STRIDE, seq // SEL_BLOCK))
    w_out_b = w_out.astype(BF16)
    n_pad = LANES - N_GROUPS - N_EXPERTS
    x2 = x.reshape(t, d)
    for l in range(depth):
        y_nsa, y_gdn, y_conf = _mix_heads(x2, l, b, seq, tables, w_in, nsa_cmp_pe, nsa_cmp_w1, nsa_cmp_w2, gdn_conv_w,
                                          gdn_a_log, gdn_dt_bias, gdn_norm_w, conf_dw_w, conf_dw_b, conf_ln_w, conf_ln_b)
        wr = jnp.concatenate([moe_w_group[l], moe_w_expert[l], jnp.zeros((d, n_pad), F32)], axis=1).astype(BF16)
        br = jnp.concatenate([moe_b_group[l], moe_b_expert[l], jnp.zeros((n_pad,), F32)]).reshape(1, LANES)
        x1, x1b, logits = _outproj(y_nsa, y_gdn, y_conf, x2, w_out_b[l], ln1_w[l].reshape(1, d),
                                   ln1_b[l].reshape(1, d), wr, br)
        x2 = _moe_block(x1, x1b, logits, l, ln2_w[l].reshape(1, d), ln2_b[l].reshape(1, d),
                        moe_w_gate, moe_w_up, moe_w_down)
    return x2.reshape(b, seq, d)
```

```python
import functools

import jax
import jax.numpy as jnp
import numpy as np
from jax import lax
from jax.experimental import pallas as pl
from jax.experimental.pallas import tpu as pltpu

F32 = jnp.float32
BF16 = jnp.bfloat16
I32 = jnp.int32

D_MODEL = 1024
DEPTH = 4
HEAD_DIM = 64
NSA_HEADS = 8
NSA_KV_HEADS = 2
NSA_GROUP = NSA_HEADS // NSA_KV_HEADS
CMP_LEN = 32
CMP_STRIDE = 16
CMP_HIDDEN = 128
SEL_BLOCK = 64
N_SEL = 16
WINDOW = 512
Q_BLOCK = 128
FORCE_BONUS = 1.0e4
ROPE_THETA = 500000.0
ROPE_DIM = HEAD_DIM // 4
GDN_HEADS = 4
GDN_CONV = 4
GDN_CHUNK = 64
CONF_CH = 256
CONF_WIDTH = 31
N_GROUPS = 4
EXPERTS_PER_GROUP = 8
N_EXPERTS = N_GROUPS * EXPERTS_PER_GROUP
EXPERT_FF = 512
NSA_WIDTH = NSA_HEADS * HEAD_DIM
KV_WIDTH = NSA_KV_HEADS * HEAD_DIM
GDN_WIDTH = GDN_HEADS * HEAD_DIM
DEEPNORM_ALPHA = (2.0 * DEPTH) ** 0.25
LN_EPS = 1e-5

LANES = 128
V7X_VMEM_BYTES = 64 * 1024 * 1024
VMEM_LIMIT = V7X_VMEM_BYTES * 3 // 4

Q_OFF = 0
KV_OFF = Q_OFF + NSA_WIDTH
GQKV_OFF = KV_OFF + 6 * KV_WIDTH
GZ_OFF = GQKV_OFF + 3 * GDN_WIDTH
CONF_OFF = GZ_OFF + GDN_WIDTH
SMALL_OFF = CONF_OFF + 2 * CONF_CH
IN_COLS = SMALL_OFF + LANES
N_SMALL = 3 * NSA_HEADS + 2 * GDN_HEADS

ROW_TILE = 512
MOE_ROWS = 512
GDN_BLOCK_CHUNKS = 4
GDN_BLOCK_BATCH = 4
SEL_KEYS = 512
NEG = -1e30


def _params(*sem):
    return pltpu.CompilerParams(dimension_semantics=sem, vmem_limit_bytes=VMEM_LIMIT)


def _dot(a, b):
    return jnp.dot(a, b, preferred_element_type=F32)


def _dot_nt(a, b):
    return lax.dot_general(a, b, (((1,), (1,)), ((), ())), preferred_element_type=F32)


def _bmm(a, b):
    return lax.dot_general(a.astype(BF16), b.astype(BF16), (((2,), (1,)), ((0,), (0,))),
                           preferred_element_type=F32)


def _bmm_nt(a, b):
    return lax.dot_general(a.astype(BF16), b.astype(BF16), (((2,), (2,)), ((0,), (0,))),
                           preferred_element_type=F32)


def _sigmoid(x):
    return 1.0 / (1.0 + jnp.exp(-x))


def _silu(x):
    return x * _sigmoid(x)


def _layer_norm(v, w, b):
    mu = jnp.mean(v, axis=-1, keepdims=True)
    d = v - mu
    var = jnp.mean(d * d, axis=-1, keepdims=True)
    return d * lax.rsqrt(var + LN_EPS) * w + b


def _inproj_kernel(x_ref, w_ref, cos_ref, sa_ref, sb_ref, oh_ref, q_ref, kvc_ref, ks_ref, kw_ref, vst_ref, vwt_ref,
                   gqkv_ref, gz_ref, conf_ref, small_ref):
    xb = x_ref[...].astype(BF16)
    cos = cos_ref[...]
    sa = sa_ref[...]
    sb = sb_ref[...]
    tm = xb.shape[0]

    def rope(h):
        return h * cos + pltpu.roll(h, LANES - ROPE_DIM // 2, 1) * sa + pltpu.roll(h, ROPE_DIM // 2, 1) * sb

    hq = _dot(xb, w_ref[:, Q_OFF:KV_OFF])
    scale = HEAD_DIM ** -0.5
    for j in range(NSA_WIDTH // LANES):
        q_ref[:, j * LANES:(j + 1) * LANES] = (rope(hq[:, j * LANES:(j + 1) * LANES]) * scale).astype(BF16)
    hkv = _dot(xb, w_ref[:, KV_OFF:GQKV_OFF])
    part = lambda j: hkv[:, j * LANES:(j + 1) * LANES]
    kvc_ref[:, 0:LANES] = rope(part(0)).astype(BF16)
    kvc_ref[:, LANES:2 * LANES] = part(1).astype(BF16)
    ksel = rope(part(2))
    first_half = lax.broadcasted_iota(I32, (tm, LANES), 1) < HEAD_DIM
    oh = oh_ref[...]
    ks_ref[:, 0:LANES] = jnp.where(first_half, ksel, oh).astype(BF16)
    ks_ref[:, LANES:2 * LANES] = jnp.where(first_half, pltpu.roll(ksel, HEAD_DIM, 1), oh).astype(BF16)
    kw_ref[...] = rope(part(4)).astype(BF16)
    vst_ref[0] = part(3).T.astype(BF16)
    vwt = part(5).T.astype(BF16)
    for c in range(tm // Q_BLOCK):
        vwt_ref[c] = vwt[:, c * Q_BLOCK:(c + 1) * Q_BLOCK]
    gqkv_ref[...] = _dot(xb, w_ref[:, GQKV_OFF:GZ_OFF])
    gz_ref[...] = _dot(xb, w_ref[:, GZ_OFF:CONF_OFF])
    conf_ref[...] = _dot(xb, w_ref[:, CONF_OFF:SMALL_OFF])
    small_ref[...] = _dot(xb, w_ref[:, SMALL_OFF:IN_COLS])


def _inproj(x2, w, cos, sa, sb, oh, seq):
    t = x2.shape[0]
    tm = ROW_TILE
    n_pos = seq // tm
    row = lambda i: (i, 0)
    pos = lambda i: (i % n_pos, 0)
    widths = (NSA_WIDTH, 2 * LANES, 2 * LANES, LANES, 3 * GDN_WIDTH, GDN_WIDTH, 2 * CONF_CH, LANES)
    dtypes = (BF16, BF16, BF16, BF16, F32, F32, F32, F32)
    flat_specs = [pl.BlockSpec((tm, wd), row) for wd in widths]
    flat_shapes = [jax.ShapeDtypeStruct((t, wd), dt) for wd, dt in zip(widths, dtypes)]
    n_q = tm // Q_BLOCK
    out_specs = flat_specs[:4] + [pl.BlockSpec((1, LANES, tm), lambda i: (i, 0, 0)),
                                  pl.BlockSpec((n_q, LANES, Q_BLOCK), lambda i: (i, 0, 0))] + flat_specs[4:]
    out_shape = flat_shapes[:4] + [jax.ShapeDtypeStruct((t // tm, LANES, tm), BF16),
                                   jax.ShapeDtypeStruct((t // Q_BLOCK, LANES, Q_BLOCK), BF16)] + flat_shapes[4:]
    return pl.pallas_call(
        _inproj_kernel,
        grid=(t // tm,),
        in_specs=[pl.BlockSpec((tm, D_MODEL), row),
                  pl.BlockSpec((D_MODEL, IN_COLS), lambda i: (0, 0)),
                  pl.BlockSpec((tm, LANES), pos), pl.BlockSpec((tm, LANES), pos), pl.BlockSpec((tm, LANES), pos),
                  pl.BlockSpec((tm, LANES), pos)],
        out_specs=out_specs,
        out_shape=out_shape,
        compiler_params=_params("parallel"),
        name="inproj",
    )(x2, w, cos, sa, sb, oh)


def _block_onehot(seq):
    blk = jnp.arange(seq)[:, None] // SEL_BLOCK
    lane = jnp.arange(LANES)[None, :]
    return jnp.where(lane - HEAD_DIM == blk, 1.0, 0.0).astype(F32)


def _rope_tables(seq):
    half = ROPE_DIM // 2
    inv = ROPE_THETA ** (-jnp.arange(0, ROPE_DIM, 2, dtype=F32) / ROPE_DIM)
    ang = jnp.arange(seq, dtype=F32)[:, None] * inv[None, :]
    c, s = jnp.cos(ang), jnp.sin(ang)
    ones = jnp.ones((seq, HEAD_DIM - ROPE_DIM), F32)
    zeros = jnp.zeros((seq, HEAD_DIM - ROPE_DIM), F32)
    zh = jnp.zeros((seq, half), F32)
    cos_h = jnp.concatenate([c, c, ones], axis=1)
    sa_h = jnp.concatenate([-s, zh, zeros], axis=1)
    sb_h = jnp.concatenate([zh, s, zeros], axis=1)
    rep = LANES // HEAD_DIM
    return jnp.tile(cos_h, (1, rep)), jnp.tile(sa_h, (1, rep)), jnp.tile(sb_h, (1, rep))


def _permute_w_in(w):
    o_q, o_kv = 0, NSA_WIDTH
    o_gate = o_kv + 6 * KV_WIDTH
    o_gqkv = o_gate + 3 * NSA_HEADS
    o_gz = o_gqkv + 3 * GDN_WIDTH
    o_b = o_gz + GDN_WIDTH
    o_a = o_b + GDN_HEADS
    o_conf = o_a + GDN_HEADS
    pad = jnp.zeros((w.shape[0], LANES - N_SMALL), w.dtype)
    return jnp.concatenate([w[:, o_q:o_gate], w[:, o_gqkv:o_b], w[:, o_conf:], w[:, o_gate:o_gqkv],
                            w[:, o_b:o_conf], pad], axis=1).astype(BF16)


def _compress_kernel(ch_ref, pet_ref, peb_ref, w1t_ref, w1b_ref, w2_ref, o_ref, ot_ref):
    ch = ch_ref[0, 0].astype(F32)
    top = _dot((ch + pet_ref[0]).astype(BF16), w1t_ref[0])
    bot = _dot((ch + peb_ref[0]).astype(BF16), w1b_ref[0])
    n = bot.shape[0]
    hid = top + pltpu.roll(bot, n - 1, 0)
    out = _dot(_silu(hid).astype(BF16), w2_ref[0])
    o_ref[0, 0] = out.astype(o_ref.dtype)
    ot_ref[0, 0] = jnp.concatenate([out, jnp.zeros_like(out)], axis=1).T[:HEAD_DIM].astype(ot_ref.dtype)


def _compress(chunks, pe_top, pe_bot, w1_top, w1_bot, w2):
    _, bh, n_chunk, feat = chunks.shape
    per = lambda r, i: (r, 0, 0)
    return pl.pallas_call(
        _compress_kernel,
        grid=(2, bh),
        in_specs=[pl.BlockSpec((1, 1, n_chunk, feat), lambda r, i: (r, i, 0, 0)),
                  pl.BlockSpec((1, 1, feat), per), pl.BlockSpec((1, 1, feat), per),
                  pl.BlockSpec((1, feat, CMP_HIDDEN), per), pl.BlockSpec((1, feat, CMP_HIDDEN), per),
                  pl.BlockSpec((1, CMP_HIDDEN, HEAD_DIM), per)],
        out_specs=[pl.BlockSpec((1, 1, n_chunk, HEAD_DIM), lambda r, i: (r, i, 0, 0)),
                   pl.BlockSpec((1, 1, HEAD_DIM, n_chunk), lambda r, i: (r, i, 0, 0))],
        out_shape=[jax.ShapeDtypeStruct((2, bh, n_chunk, HEAD_DIM), BF16),
                   jax.ShapeDtypeStruct((2, bh, HEAD_DIM, n_chunk), BF16)],
        compiler_params=_params("parallel", "parallel"),
        name="nsa_compress",
    )(chunks, pe_top, pe_bot, w1_top, w1_bot, w2)


def _nsa_kernel(q_ref, kc_ref, vct_ref, ks_ref, vst_ref, kw_ref, vwt_ref, gl_ref, ovt_ref, o_ref, *, seq, n_sel):
    g, qn = NSA_GROUP, Q_BLOCK
    lanes = g * qn
    n_c = seq // CMP_STRIDE
    head = pl.program_id(1)
    qb = pl.program_id(2)
    s0 = qb * qn
    tile = lambda a: jnp.concatenate([a] * g, axis=1)
    qt4 = q_ref[...].astype(F32).T
    q_t = jnp.concatenate([qt4[i * HEAD_DIM:(i + 1) * HEAD_DIM] for i in range(g)], axis=1).astype(BF16)
    t_row = s0 + (lax.broadcasted_iota(I32, (1, lanes), 1) & (qn - 1))
    ones_rows = jnp.ones((16, ROW_TILE), BF16)

    sc = _dot(kc_ref[0, 0], q_t)
    cend = lax.broadcasted_iota(I32, (n_c, lanes), 0) * CMP_STRIDE + (CMP_LEN - 1)
    sc = jnp.where(cend <= t_row, sc, -jnp.inf)
    m = jnp.max(sc, axis=0, keepdims=True)
    m = jnp.where(m == -jnp.inf, 0.0, m)
    e = jnp.exp(sc - m)
    pc = e * (1.0 / jnp.maximum(jnp.sum(e, axis=0, keepdims=True), 1e-30))
    o_cmp = _dot(vct_ref[0, 0], pc.astype(BF16))

    pcs = pc[:, 0:qn]
    for i in range(1, g):
        pcs = pcs + pc[:, i * qn:(i + 1) * qn]
    imp = jnp.dot(ovt_ref[...], pcs, preferred_element_type=F32, precision=lax.Precision.HIGHEST)
    n_blk = imp.shape[0]
    blk = lax.broadcasted_iota(I32, (n_blk, qn), 0)
    cur = (s0 + lax.broadcasted_iota(I32, (n_blk, qn), 1)) // SEL_BLOCK
    forced = (blk == 0) | (blk == cur) | (blk == cur - 1)
    causal = blk <= cur
    imp = jnp.where(causal, imp + jnp.where(forced, FORCE_BONUS, 0.0), -jnp.inf)
    rank = jnp.zeros((n_blk, qn), F32)
    for i in range(n_blk):
        vi = imp[i:i + 1, :]
        ahead = (vi > imp) | ((vi == imp) & (blk > i))
        rank = rank + jnp.where(ahead, 1.0, 0.0)
    sel_bias = jnp.where((rank < n_sel) & causal, 0.0, NEG).astype(BF16)
    q_aug = jnp.concatenate([q_t, tile(sel_bias)], axis=0)

    zeros = jnp.zeros_like(q_t)
    q_win = jnp.where(head == 0, jnp.concatenate([q_t, zeros], axis=0), jnp.concatenate([zeros, q_t], axis=0))
    n_wb = WINDOW // qn + 1
    sw = []
    for kb in range(n_wb):
        bi = qb - (n_wb - 1) + kb
        k0 = pl.multiple_of(jnp.maximum(bi, 0) * qn, qn)
        pos = bi * qn + lax.broadcasted_iota(I32, (qn, qn), 0)
        tq = s0 + lax.broadcasted_iota(I32, (qn, qn), 1)
        ok = (pos >= 0) & (pos <= tq) & (pos > tq - WINDOW)
        sw.append(_dot(kw_ref[pl.ds(k0, qn), :], q_win) + tile(jnp.where(ok, 0.0, NEG)))
    m_w = jnp.max(sw[0], axis=0, keepdims=True)
    for kb in range(1, n_wb):
        m_w = jnp.maximum(m_w, jnp.max(sw[kb], axis=0, keepdims=True))
    acc_w = jnp.zeros((HEAD_DIM + 16, lanes), F32)
    for kb in range(n_wb):
        vt = jnp.concatenate([vwt_ref[0, jnp.maximum(qb - (n_wb - 1) + kb, 0)], ones_rows[:, 0:qn]], axis=0)
        acc_w = acc_w + _dot(vt, jnp.exp(sw[kb] - m_w).astype(BF16))
    o_win = acc_w[0:HEAD_DIM] * (1.0 / acc_w[HEAD_DIM:HEAD_DIM + 1])

    kstep = ROW_TILE
    c_last = s0 // kstep

    def sel_scores(c):
        k0 = pl.multiple_of(c * kstep, kstep)
        return _dot(ks_ref[pl.ds(k0, kstep), :], q_aug)

    def sel_values(c):
        return jnp.concatenate([vst_ref[0, c], ones_rows], axis=0)

    def sel_update(c, m_i, acc, s):
        m_new = jnp.maximum(m_i, jnp.max(s, axis=0, keepdims=True))
        p = jnp.exp(s - m_new).astype(BF16)
        return m_new, jnp.exp(m_i - m_new) * acc + _dot(sel_values(c), p)

    def sel_step(c, carry):
        m_i, acc, s = carry
        s_next = sel_scores(c + 1)
        return sel_update(c, m_i, acc, s) + (s_next,)

    init = (jnp.full((1, lanes), NEG, F32), jnp.zeros((HEAD_DIM + 16, lanes), F32), sel_scores(0))
    m_s, acc_s, s_last = lax.fori_loop(0, c_last, sel_step, init)
    key = c_last * kstep + lax.broadcasted_iota(I32, (kstep, qn), 0)
    future = jnp.where(key <= s0 + lax.broadcasted_iota(I32, (kstep, qn), 1), 0.0, NEG)
    _, acc_s = sel_update(c_last, m_s, acc_s, s_last + tile(future))
    o_sel = acc_s[0:HEAD_DIM] * (1.0 / acc_s[HEAD_DIM:HEAD_DIM + 1])

    gate = _sigmoid(gl_ref[0, 0, 0])
    o = o_cmp * gate[0:1] + o_sel * gate[1:2] + o_win * gate[2:3]
    o4 = jnp.concatenate([o[:, i * qn:(i + 1) * qn] for i in range(g)], axis=0)
    o_ref[...] = o4.T.astype(o_ref.dtype)


def _overlap_t(n_c, n_sb):
    c0 = jnp.arange(n_c)[None, :] * CMP_STRIDE
    b0 = jnp.arange(HEAD_DIM)[:, None] * SEL_BLOCK
    ov = jnp.minimum(c0 + CMP_LEN, b0 + SEL_BLOCK) - jnp.maximum(c0, b0)
    ov = (jnp.maximum(ov, 0) / CMP_STRIDE).astype(F32)
    return jnp.where(jnp.arange(HEAD_DIM)[:, None] < n_sb, ov, 0.0)


def _nsa_attention(q, kc, vct, ks, vst, kw, vwt, gl, ov_t, b, seq):
    n_c = seq // CMP_STRIDE
    n_sb = seq // SEL_BLOCK
    n_qb = seq // Q_BLOCK
    n_sel = min(N_SEL, n_sb)
    t = b * seq
    width = NSA_GROUP * HEAD_DIM
    vst4 = vst.reshape(b, seq // ROW_TILE, LANES, ROW_TILE)
    vwt4 = vwt.reshape(b, n_qb, LANES, Q_BLOCK)
    return pl.pallas_call(
        functools.partial(_nsa_kernel, seq=seq, n_sel=n_sel),
        grid=(b, NSA_KV_HEADS, n_qb),
        in_specs=[pl.BlockSpec((Q_BLOCK, width), lambda bi, h, i: (bi * n_qb + i, h)),
                  pl.BlockSpec((1, 1, n_c, HEAD_DIM), lambda bi, h, i: (0, bi * NSA_KV_HEADS + h, 0, 0)),
                  pl.BlockSpec((1, 1, HEAD_DIM, n_c), lambda bi, h, i: (1, bi * NSA_KV_HEADS + h, 0, 0)),
                  pl.BlockSpec((seq, LANES), lambda bi, h, i: (bi, h)),
                  pl.BlockSpec((1, seq // ROW_TILE, HEAD_DIM, ROW_TILE), lambda bi, h, i: (bi, 0, h, 0)),
                  pl.BlockSpec((seq, LANES), lambda bi, h, i: (bi, 0)),
                  pl.BlockSpec((1, n_qb, HEAD_DIM, Q_BLOCK), lambda bi, h, i: (bi, 0, h, 0)),
                  pl.BlockSpec((1, 1, 1, 3, NSA_GROUP * Q_BLOCK), lambda bi, h, i: (bi, h, i, 0, 0)),
                  pl.BlockSpec((HEAD_DIM, n_c), lambda bi, h, i: (0, 0))],
        out_specs=pl.BlockSpec((Q_BLOCK, width), lambda bi, h, i: (bi * n_qb + i, h)),
        out_shape=jax.ShapeDtypeStruct((t, NSA_WIDTH), BF16),
        compiler_params=_params("parallel", "parallel", "arbitrary"),
        name="nsa_attention",
    )(q, kc, vct, ks, vst4, kw, vwt4, gl, ov_t)


def _gdn_conv_kernel(cur_ref, prev_ref, w_ref, o_ref, ext_ref):
    ts = cur_ref.shape[1]
    halo = prev_ref.shape[1]
    first = pl.program_id(1) == 0
    ext_ref[0:halo, :] = jnp.where(first, 0.0, prev_ref[0])
    ext_ref[halo:halo + ts, :] = cur_ref[0]
    acc = cur_ref[0] * w_ref[GDN_CONV - 1:GDN_CONV, :]
    for k in range(GDN_CONV - 1):
        acc = acc + ext_ref[pl.ds(halo - (GDN_CONV - 1) + k, ts), :] * w_ref[k:k + 1, :]
    o_ref[0] = _silu(acc)


def _gdn_conv(x3, w):
    b, seq, ch = x3.shape
    ts = min(ROW_TILE, seq)
    halo = 8
    per_tile = ts // halo
    return pl.pallas_call(
        _gdn_conv_kernel,
        grid=(b, seq // ts),
        in_specs=[pl.BlockSpec((1, ts, ch), lambda bi, i: (bi, i, 0)),
                  pl.BlockSpec((1, halo, ch), lambda bi, i: (bi, jnp.maximum(i * per_tile - 1, 0), 0)),
                  pl.BlockSpec((GDN_CONV, ch), lambda bi, i: (0, 0))],
        out_specs=pl.BlockSpec((1, ts, ch), lambda bi, i: (bi, i, 0)),
        out_shape=jax.ShapeDtypeStruct((b, seq, ch), F32),
        scratch_shapes=[pltpu.VMEM((ts + halo, ch), F32)],
        compiler_params=_params("parallel", "parallel"),
        name="gdn_conv",
    )(x3, x3, w)


def _l2norm(v):
    return v * lax.rsqrt(jnp.sum(v * v, axis=-1, keepdims=True) + 1e-6)


def _softplus(v):
    return jnp.maximum(v, 0.0) + jnp.log1p(jnp.exp(-jnp.abs(v)))


def _gdn_kernel(q_ref, k_ref, v_ref, z_ref, b_ref, ac_ref, ar_ref, alog_ref, dtb_ref, nw_ref, o_ref, state_ref,
                *, chunks):
    c_len = GDN_CHUNK

    @pl.when(pl.program_id(1) == 0)
    def _():
        state_ref[...] = jnp.zeros_like(state_ref)

    ri = lax.broadcasted_iota(I32, (c_len, c_len), 0)
    ci = lax.broadcasted_iota(I32, (c_len, c_len), 1)
    lower = (ri >= ci)[None]
    strict = (ri > ci)[None]
    upper = (ri <= ci)[None]
    eye = jnp.where(ri == ci, 1.0, 0.0)[None]
    nb, nh = z_ref.shape[0], z_ref.shape[1]
    n = nb * nh
    rate = jnp.concatenate([-jnp.exp(alog_ref[...])] * nb, axis=0)
    dtb = jnp.concatenate([dtb_ref[...]] * nb, axis=0)
    nw = nw_ref[...]

    def chunk(c, carry):
        r0 = pl.multiple_of(c * c_len, c_len)
        rows = pl.ds(r0, c_len)
        q = _l2norm(q_ref[0, :, :, rows, :].reshape(n, c_len, HEAD_DIM)) * HEAD_DIM ** -0.5
        k = _l2norm(k_ref[0, :, :, rows, :].reshape(n, c_len, HEAD_DIM))
        v = v_ref[0, :, :, rows, :].reshape(n, c_len, HEAD_DIM)
        beta = _sigmoid(b_ref[:, :, rows, :].reshape(n, c_len, 1))
        g_col = rate * _softplus(ac_ref[:, :, rows, :].reshape(n, c_len, 1) + dtb)
        g_row = rate * _softplus(ar_ref[:, :, c].reshape(n, 1, c_len) + dtb)
        gc_col = jnp.sum(jnp.where(lower, g_row, 0.0), axis=2, keepdims=True)
        gc_row = jnp.sum(jnp.where(upper, g_col, 0.0), axis=1, keepdims=True)
        decay = jnp.exp(jnp.where(lower, gc_col - gc_row, -jnp.inf))
        kb = k * beta
        a = jnp.where(strict, _bmm_nt(kb, k) * decay, 0.0)
        inv = eye - a
        pw = a
        for _ in range(int(np.log2(c_len)) - 1):
            pw = _bmm(pw, pw)
            inv = inv + _bmm(inv, pw)
        e_col = jnp.exp(gc_col)
        u = _bmm(inv, v * beta)
        w = _bmm(inv, kb * e_col)
        qk = _bmm_nt(q, k) * decay
        g_last = gc_col[:, c_len - 1:c_len, :]
        k_dec = k * jnp.exp(g_last - gc_col)
        state = state_ref[...]
        v_new = u - _bmm(w, state)
        o = _bmm(q * e_col, state) + _bmm(qk, v_new)
        state_ref[...] = state * jnp.exp(g_last) + _bmm(jnp.swapaxes(k_dec, 1, 2), v_new)
        o = o * lax.rsqrt(jnp.mean(o * o, axis=-1, keepdims=True) + 1e-6) * nw
        o = o * _silu(z_ref[:, :, rows, :].reshape(n, c_len, HEAD_DIM))
        o_ref[:, :, rows, :] = o.reshape(nb, nh, c_len, HEAD_DIM).astype(o_ref.dtype)
        return carry

    lax.fori_loop(0, chunks, chunk, 0)


def _gdn(qkvh, zh, b_col, a_col, a_row, a_log, dt_bias, norm_w):
    _, b, h, seq, dh = qkvh.shape
    chunks = GDN_BLOCK_CHUNKS
    rows = chunks * GDN_CHUNK
    nb = GDN_BLOCK_BATCH if b % GDN_BLOCK_BATCH == 0 else 1

    def qkv_spec(r):
        return pl.BlockSpec((1, nb, h, rows, dh), lambda bi, i: (r, bi, 0, i, 0))

    col = pl.BlockSpec((nb, h, rows, 1), lambda bi, i: (bi, 0, i, 0))
    head = pl.BlockSpec((h, 1, 1), lambda bi, i: (0, 0, 0))
    return pl.pallas_call(
        functools.partial(_gdn_kernel, chunks=chunks),
        grid=(b // nb, seq // rows),
        in_specs=[qkv_spec(0), qkv_spec(1), qkv_spec(2),
                  pl.BlockSpec((nb, h, rows, dh), lambda bi, i: (bi, 0, i, 0)),
                  col, col,
                  pl.BlockSpec((nb, h, chunks, 1, GDN_CHUNK), lambda bi, i: (bi, 0, i, 0, 0)),
                  head, head,
                  pl.BlockSpec((1, dh), lambda bi, i: (0, 0))],
        out_specs=pl.BlockSpec((nb, h, rows, dh), lambda bi, i: (bi, 0, i, 0)),
        out_shape=jax.ShapeDtypeStruct((b, h, seq, dh), BF16),
        scratch_shapes=[pltpu.VMEM((nb * h, dh, dh), F32)],
        compiler_params=_params("parallel", "arbitrary"),
        name="gdn_delta_rule",
    )(qkvh, qkvh, qkvh, zh, b_col, a_col, a_row, a_log, dt_bias, norm_w)


def _conf_kernel(cur_ref, prev_ref, w_ref, b_ref, lnw_ref, lnb_ref, o_ref, ext_ref):
    ts = cur_ref.shape[1]
    halo = prev_ref.shape[1]
    first = pl.program_id(1) == 0

    def glu(blk):
        return blk[:, :CONF_CH] * _sigmoid(blk[:, CONF_CH:])

    ext_ref[0:halo, :] = jnp.where(first, 0.0, glu(prev_ref[0]))
    ext_ref[halo:halo + ts, :] = glu(cur_ref[0])
    acc = jnp.zeros((ts, CONF_CH), F32) + b_ref[...]
    for k in range(CONF_WIDTH):
        acc = acc + ext_ref[pl.ds(halo - (CONF_WIDTH - 1) + k, ts), :] * w_ref[k:k + 1, :]
    o_ref[0] = _silu(_layer_norm(acc, lnw_ref[...], lnb_ref[...])).astype(o_ref.dtype)


def _conformer(x3, w, bias, ln_w, ln_b):
    b, seq, ch = x3.shape
    ts = min(ROW_TILE, seq)
    halo = 32
    per_tile = ts // halo
    vec = pl.BlockSpec((1, CONF_CH), lambda bi, i: (0, 0))
    return pl.pallas_call(
        _conf_kernel,
        grid=(b, seq // ts),
        in_specs=[pl.BlockSpec((1, ts, ch), lambda bi, i: (bi, i, 0)),
                  pl.BlockSpec((1, halo, ch), lambda bi, i: (bi, jnp.maximum(i * per_tile - 1, 0), 0)),
                  pl.BlockSpec((CONF_WIDTH, CONF_CH), lambda bi, i: (0, 0)), vec, vec, vec],
        out_specs=pl.BlockSpec((1, ts, CONF_CH), lambda bi, i: (bi, i, 0)),
        out_shape=jax.ShapeDtypeStruct((b, seq, CONF_CH), BF16),
        scratch_shapes=[pltpu.VMEM((ts + halo, CONF_CH), F32)],
        compiler_params=_params("parallel", "parallel"),
        name="conformer_conv",
    )(x3, x3, w, bias, ln_w, ln_b)


def _outproj_kernel(yn_ref, yg_ref, yc_ref, x_ref, w_ref, lnw_ref, lnb_ref, wr_ref, br_ref, x1_ref, x1b_ref, lg_ref):
    mix = _dot(yn_ref[...], w_ref[0:NSA_WIDTH, :])
    mix = mix + _dot(yg_ref[...], w_ref[NSA_WIDTH:NSA_WIDTH + GDN_WIDTH, :])
    mix = mix + _dot(yc_ref[...], w_ref[NSA_WIDTH + GDN_WIDTH:, :])
    x1 = _layer_norm(DEEPNORM_ALPHA * x_ref[...] + mix, lnw_ref[...], lnb_ref[...])
    x1_ref[...] = x1
    xb = x1.astype(BF16)
    x1b_ref[...] = xb
    lg_ref[...] = _dot(xb, wr_ref[...]) + br_ref[...]


def _outproj(y_nsa, y_gdn, y_conf, x2, w, ln_w, ln_b, wr, br):
    t = x2.shape[0]
    tm = min(ROW_TILE, t)
    row = lambda i: (i, 0)
    fixed = lambda i: (0, 0)
    return pl.pallas_call(
        _outproj_kernel,
        grid=(t // tm,),
        in_specs=[pl.BlockSpec((tm, NSA_WIDTH), row), pl.BlockSpec((tm, GDN_WIDTH), row),
                  pl.BlockSpec((tm, CONF_CH), row), pl.BlockSpec((tm, D_MODEL), row),
                  pl.BlockSpec((D_MODEL, D_MODEL), fixed), pl.BlockSpec((1, D_MODEL), fixed),
                  pl.BlockSpec((1, D_MODEL), fixed), pl.BlockSpec((D_MODEL, LANES), fixed),
                  pl.BlockSpec((1, LANES), fixed)],
        out_specs=[pl.BlockSpec((tm, D_MODEL), row), pl.BlockSpec((tm, D_MODEL), row), pl.BlockSpec((tm, LANES), row)],
        out_shape=[jax.ShapeDtypeStruct((t, D_MODEL), F32), jax.ShapeDtypeStruct((t, D_MODEL), BF16),
                   jax.ShapeDtypeStruct((t, LANES), F32)],
        compiler_params=_params("parallel"),
        name="outproj_ln_router",
    )(y_nsa, y_gdn, y_conf, x2, w, ln_w, ln_b, wr, br)


def _route_kernel(lg_ref, route_ref, counts_ref, carry_ref):
    tm = lg_ref.shape[0]

    @pl.when(pl.program_id(0) == 0)
    def _():
        carry_ref[...] = jnp.zeros_like(carry_ref)

    lg = lg_ref[...]
    lane = lax.broadcasted_iota(I32, (tm, LANES), 1)
    lane_f = lane.astype(F32)

    def top1(vals):
        best = jnp.max(vals, axis=1, keepdims=True)
        idx = jnp.min(jnp.where(vals == best, lane_f, float(LANES)), axis=1, keepdims=True)
        return best, idx

    gl = jnp.where(lane < N_GROUPS, lg, -jnp.inf)
    g_best, g_idx = top1(gl)
    g_w = 1.0 / jnp.sum(jnp.exp(gl - g_best), axis=1, keepdims=True)
    lo = N_GROUPS + g_idx * EXPERTS_PER_GROUP
    el = jnp.where((lane_f >= lo) & (lane_f < lo + EXPERTS_PER_GROUP), lg, -jnp.inf)
    e1, i1 = top1(el)
    e2, i2 = top1(jnp.where(lane_f == i1, -jnp.inf, el))
    r = jnp.exp(e2 - e1)
    w1 = g_w / (1.0 + r)
    w2 = g_w * r / (1.0 + r)
    hit1 = lane_f == i1
    hit2 = lane_f == i2
    onehot = jnp.where(hit1 | hit2, 1.0, 0.0)
    ri = lax.broadcasted_iota(I32, (tm, tm), 0)
    ci = lax.broadcasted_iota(I32, (tm, tm), 1)
    before = jnp.where(ri > ci, 1.0, 0.0).astype(BF16)
    seen = carry_ref[...] + _dot(before, onehot.astype(BF16))
    p1 = jnp.sum(jnp.where(hit1, seen, 0.0), axis=1, keepdims=True)
    p2 = jnp.sum(jnp.where(hit2, seen, 0.0), axis=1, keepdims=True)
    total = carry_ref[...] + jnp.sum(onehot, axis=0, keepdims=True)
    carry_ref[...] = total
    counts_ref[...] = total
    cols = (i1 - N_GROUPS, i2 - N_GROUPS, w1, w2, p1, p2)
    out = jnp.zeros((tm, LANES), F32)
    for j, cval in enumerate(cols):
        out = jnp.where(lane == j, cval, out)
    route_ref[...] = out


def _route(logits):
    t = logits.shape[0]
    tm = min(ROW_TILE, t)
    return pl.pallas_call(
        _route_kernel,
        grid=(t // tm,),
        in_specs=[pl.BlockSpec((tm, LANES), lambda i: (i, 0))],
        out_specs=[pl.BlockSpec((tm, LANES), lambda i: (i, 0)), pl.BlockSpec((1, LANES), lambda i: (0, 0))],
        out_shape=[jax.ShapeDtypeStruct((t, LANES), F32), jax.ShapeDtypeStruct((1, LANES), F32)],
        scratch_shapes=[pltpu.VMEM((1, LANES), F32)],
        compiler_params=_params("arbitrary"),
        name="moe_route",
    )(logits)


def _expert_kernel(be_ref, nu_ref, x_ref, wg_ref, wu_ref, wd_ref, o_ref, wgb_ref, wub_ref, wdb_ref):
    i = pl.program_id(0)
    changed = (i == 0) | (be_ref[i] != be_ref[jnp.maximum(i - 1, 0)])

    @pl.when(changed)
    def _():
        wgb_ref[...] = wg_ref[0, 0].astype(BF16)
        wub_ref[...] = wu_ref[0, 0].astype(BF16)
        wdb_ref[...] = wd_ref[0, 0].astype(BF16)

    @pl.when(i < nu_ref[0])
    def _():
        xb = x_ref[...]
        hid = _silu(_dot(xb, wgb_ref[...])) * _dot(xb, wub_ref[...])
        o_ref[...] = _dot(hid.astype(BF16), wdb_ref[...])

    @pl.when(i >= nu_ref[0])
    def _():
        o_ref[...] = jnp.zeros_like(o_ref)


def _experts(blk_expert, n_used, buf, w_gate, w_up, w_down, layer):
    rows = buf.shape[0]
    n_blocks = rows // MOE_ROWS
    wspec_in = pl.BlockSpec((1, 1, D_MODEL, EXPERT_FF), lambda i, be, nu: (layer, be[i], 0, 0))
    wspec_out = pl.BlockSpec((1, 1, EXPERT_FF, D_MODEL), lambda i, be, nu: (layer, be[i], 0, 0))
    return pl.pallas_call(
        _expert_kernel,
        grid_spec=pltpu.PrefetchScalarGridSpec(
            num_scalar_prefetch=2,
            grid=(n_blocks,),
            in_specs=[pl.BlockSpec((MOE_ROWS, D_MODEL), lambda i, be, nu: (i, 0)), wspec_in, wspec_in, wspec_out],
            out_specs=pl.BlockSpec((MOE_ROWS, D_MODEL), lambda i, be, nu: (i, 0)),
            scratch_shapes=[pltpu.VMEM((D_MODEL, EXPERT_FF), BF16), pltpu.VMEM((D_MODEL, EXPERT_FF), BF16),
                            pltpu.VMEM((EXPERT_FF, D_MODEL), BF16)]),
        out_shape=jax.ShapeDtypeStruct((rows, D_MODEL), F32),
        compiler_params=_params("arbitrary"),
        name="moe_experts",
    )(blk_expert, n_used, buf, w_gate, w_up, w_down)


def _combine_kernel(x1_ref, y0_ref, y1_ref, route_ref, lnw_ref, lnb_ref, o_ref):
    route = route_ref[...]
    moe = y0_ref[...] * route[:, 2:3] + y1_ref[...] * route[:, 3:4]
    o_ref[...] = _layer_norm(DEEPNORM_ALPHA * x1_ref[...] + moe, lnw_ref[...], lnb_ref[...])


def _combine(x1, y0, y1, route, ln_w, ln_b):
    t = x1.shape[0]
    tm = min(ROW_TILE, t)
    row = lambda i: (i, 0)
    fixed = lambda i: (0, 0)
    big = pl.BlockSpec((tm, D_MODEL), row)
    return pl.pallas_call(
        _combine_kernel,
        grid=(t // tm,),
        in_specs=[big, big, big, pl.BlockSpec((tm, LANES), row),
                  pl.BlockSpec((1, D_MODEL), fixed), pl.BlockSpec((1, D_MODEL), fixed)],
        out_specs=big,
        out_shape=jax.ShapeDtypeStruct((t, D_MODEL), F32),
        compiler_params=_params("parallel"),
        name="moe_combine_ln",
    )(x1, y0, y1, route, ln_w, ln_b)


def _mix_heads(x2, l, b, seq, tables, w_in, nsa_cmp_pe, nsa_cmp_w1, nsa_cmp_w2, gdn_conv_w, gdn_a_log, gdn_dt_bias,
               gdn_norm_w, conf_dw_w, conf_dw_b, conf_ln_w, conf_ln_b):
    cos, sa, sb, oh, ov_t = tables
    t = b * seq
    n_c = seq // CMP_STRIDE
    n_gc = seq // GDN_CHUNK
    n_qb = seq // Q_BLOCK
    half = CMP_LEN // 2
    feat = CMP_STRIDE * HEAD_DIM
    q, kvc, ks, kw, vst, vwt, gqkv, gz, conf_in, small = _inproj(x2, _permute_w_in(w_in[l]), cos, sa, sb, oh, seq)

    chunks = kvc.reshape(b, seq, 2, NSA_KV_HEADS, HEAD_DIM).transpose(2, 0, 3, 1, 4)
    chunks = chunks.reshape(2, b * NSA_KV_HEADS, n_c, feat)
    pe = nsa_cmp_pe[l]
    kc, vct = _compress(chunks, pe[:, :half].reshape(2, 1, feat), pe[:, half:].reshape(2, 1, feat),
                        nsa_cmp_w1[l][:, :feat].astype(BF16), nsa_cmp_w1[l][:, feat:].astype(BF16),
                        nsa_cmp_w2[l].astype(BF16))
    gl = small[:, :3 * NSA_HEADS].reshape(b, n_qb, Q_BLOCK, NSA_KV_HEADS, NSA_GROUP, 3)
    gl = gl.transpose(0, 3, 1, 5, 4, 2).reshape(b, NSA_KV_HEADS, n_qb, 3, NSA_GROUP * Q_BLOCK)
    y_nsa = _nsa_attention(q, kc, vct, ks, vst, kw, vwt, gl, ov_t, b, seq)

    conv = _gdn_conv(gqkv.reshape(b, seq, 3 * GDN_WIDTH), gdn_conv_w[l])
    qkvh = conv.reshape(b, seq, 3, GDN_HEADS, HEAD_DIM).transpose(2, 0, 3, 1, 4)
    zh = gz.reshape(b, seq, GDN_HEADS, HEAD_DIM).transpose(0, 2, 1, 3)
    o_b = 3 * NSA_HEADS
    b_t = small[:, o_b:o_b + GDN_HEADS].reshape(b, seq, GDN_HEADS).transpose(0, 2, 1)
    a_t = small[:, o_b + GDN_HEADS:o_b + 2 * GDN_HEADS].reshape(b, seq, GDN_HEADS).transpose(0, 2, 1)
    y_gdn = _gdn(qkvh, zh, b_t[..., None], a_t[..., None], a_t.reshape(b, GDN_HEADS, n_gc, 1, GDN_CHUNK),
                 gdn_a_log[l].reshape(GDN_HEADS, 1, 1), gdn_dt_bias[l].reshape(GDN_HEADS, 1, 1),
                 gdn_norm_w[l].reshape(1, HEAD_DIM))
    y_gdn = y_gdn.transpose(0, 2, 1, 3).reshape(t, GDN_WIDTH)

    y_conf = _conformer(conf_in.reshape(b, seq, 2 * CONF_CH), conf_dw_w[l], conf_dw_b[l].reshape(1, CONF_CH),
                        conf_ln_w[l].reshape(1, CONF_CH), conf_ln_b[l].reshape(1, CONF_CH)).reshape(t, CONF_CH)
    return y_nsa, y_gdn, y_conf


def _moe_block(x1, x1b, logits, l, ln_w, ln_b, moe_w_gate, moe_w_up, moe_w_down):
    t = x1.shape[0]
    tk = 2 * t
    n_blocks = tk // MOE_ROWS + N_EXPERTS
    route, counts = _route(logits)
    eid = route[:, 0:2].astype(I32)
    pos = route[:, 4:6].astype(I32)
    cnt = counts[0, N_GROUPS:N_GROUPS + N_EXPERTS].astype(I32)
    padded = (cnt + MOE_ROWS - 1) // MOE_ROWS * MOE_ROWS
    pad_end = jnp.cumsum(padded)
    pad_start = pad_end - padded
    first_row = jnp.sum(jnp.where(eid[..., None] == jnp.arange(N_EXPERTS, dtype=I32), pad_start, 0), axis=-1)
    dest = first_row + pos
    spare = jnp.arange(n_blocks * MOE_ROWS, dtype=I32) % t
    src = spare.at[dest.reshape(tk)].set(jnp.arange(tk, dtype=I32) // 2, unique_indices=True,
                                         mode='promise_in_bounds')
    blk_row = jnp.arange(n_blocks, dtype=I32)[:, None] * MOE_ROWS
    blk_expert = jnp.minimum(jnp.sum((pad_end[None, :] <= blk_row).astype(I32), axis=1), N_EXPERTS - 1)
    n_used = (pad_end[-1:] // MOE_ROWS).astype(I32)
    rows_of = lambda a, idx: a.at[idx].get(mode='promise_in_bounds')
    y_buf = _experts(blk_expert, n_used, rows_of(x1b, src), moe_w_gate, moe_w_up, moe_w_down, l)
    return _combine(x1, rows_of(y_buf, dest[:, 0]), rows_of(y_buf, dest[:, 1]), route, ln_w, ln_b)


def kernel(x, w_in, w_out, nsa_cmp_pe, nsa_cmp_w1, nsa_cmp_w2, gdn_conv_w, gdn_a_log, gdn_dt_bias, gdn_norm_w, conf_dw_w, conf_dw_b, conf_ln_w, conf_ln_b, ln1_w, ln1_b, ln2_w, ln2_b, moe_w_group, moe_b_group, moe_w_expert, moe_b_expert, moe_w_gate, moe_w_up, moe_w_down):
    b, seq, d = x.shape
    depth = w_in.shape[0]
    assert d == D_MODEL and seq % ROW_TILE == 0 and seq >= WINDOW + Q_BLOCK
    t = b * seq
    assert seq // SEL_BLOCK <= HEAD_DIM
    tables = _rope_tables(seq) + (_block_onehot(seq), _overlap_t(seq // CMP_STRIDE, seq // SEL_BLOCK))
    w_out_b = w_out.astype(BF16)
    n_pad = LANES - N_GROUPS - N_EXPERTS
    x2 = x.reshape(t, d)
    for l in range(depth):
        y_nsa, y_gdn, y_conf = _mix_heads(x2, l, b, seq, tables, w_in, nsa_cmp_pe, nsa_cmp_w1, nsa_cmp_w2, gdn_conv_w,
                                          gdn_a_log, gdn_dt_bias, gdn_norm_w, conf_dw_w, conf_dw_b, conf_ln_w, conf_ln_b)
        wr = jnp.concatenate([moe_w_group[l], moe_w_expert[l], jnp.zeros((d, n_pad), F32)], axis=1).astype(BF16)
        br = jnp.concatenate([moe_b_group[l], moe_b_expert[l], jnp.zeros((n_pad,), F32)]).reshape(1, LANES)
        x1, x1b, logits = _outproj(y_nsa, y_gdn, y_conf, x2, w_out_b[l], ln1_w[l].reshape(1, d),
                                   ln1_b[l].reshape(1, d), wr, br)
        x2 = _moe_block(x1, x1b, logits, l, ln2_w[l].reshape(1, d), ln2_b[l].reshape(1, d),
                        moe_w_gate, moe_w_up, moe_w_down)
    return x2.reshape(b, seq, d)
```

```python
import functools

import jax
import jax.numpy as jnp
import numpy as np
from jax import lax
from jax.experimental import pallas as pl
from jax.experimental.pallas import tpu as pltpu

F32 = jnp.float32
BF16 = jnp.bfloat16
I32 = jnp.int32

D_MODEL = 1024
DEPTH = 4
HEAD_DIM = 64
NSA_HEADS = 8
NSA_KV_HEADS = 2
NSA_GROUP = NSA_HEADS // NSA_KV_HEADS
CMP_LEN = 32
CMP_STRIDE = 16
CMP_HIDDEN = 128
SEL_BLOCK = 64
N_SEL = 16
WINDOW = 512
Q_BLOCK = 128
FORCE_BONUS = 1.0e4
ROPE_THETA = 500000.0
ROPE_DIM = HEAD_DIM // 4
GDN_HEADS = 4
GDN_CONV = 4
GDN_CHUNK = 64
CONF_CH = 256
CONF_WIDTH = 31
N_GROUPS = 4
EXPERTS_PER_GROUP = 8
N_EXPERTS = N_GROUPS * EXPERTS_PER_GROUP
EXPERT_FF = 512
NSA_WIDTH = NSA_HEADS * HEAD_DIM
KV_WIDTH = NSA_KV_HEADS * HEAD_DIM
GDN_WIDTH = GDN_HEADS * HEAD_DIM
DEEPNORM_ALPHA = (2.0 * DEPTH) ** 0.25
LN_EPS = 1e-5

LANES = 128
V7X_VMEM_BYTES = 64 * 1024 * 1024
VMEM_LIMIT = V7X_VMEM_BYTES * 3 // 4

Q_OFF = 0
KV_OFF = Q_OFF + NSA_WIDTH
GQKV_OFF = KV_OFF + 6 * KV_WIDTH
GZ_OFF = GQKV_OFF + 3 * GDN_WIDTH
CONF_OFF = GZ_OFF + GDN_WIDTH
SMALL_OFF = CONF_OFF + 2 * CONF_CH
IN_COLS = SMALL_OFF + LANES
N_SMALL = 3 * NSA_HEADS + 2 * GDN_HEADS

ROW_TILE = 512
MOE_ROWS = 512
GDN_BLOCK_CHUNKS = 4
GDN_BLOCK_BATCH = 4
SEL_KEYS = 512
NEG = -1e30


def _params(*sem):
    return pltpu.CompilerParams(dimension_semantics=sem, vmem_limit_bytes=VMEM_LIMIT)


def _dot(a, b):
    return jnp.dot(a, b, preferred_element_type=F32)


def _dot_nt(a, b):
    return lax.dot_general(a, b, (((1,), (1,)), ((), ())), preferred_element_type=F32)


def _bmm(a, b):
    return lax.dot_general(a.astype(BF16), b.astype(BF16), (((2,), (1,)), ((0,), (0,))),
                           preferred_element_type=F32)


def _bmm_nt(a, b):
    return lax.dot_general(a.astype(BF16), b.astype(BF16), (((2,), (2,)), ((0,), (0,))),
                           preferred_element_type=F32)


def _sigmoid(x):
    return 1.0 / (1.0 + jnp.exp(-x))


def _silu(x):
    return x * _sigmoid(x)


def _layer_norm(v, w, b):
    mu = jnp.mean(v, axis=-1, keepdims=True)
    d = v - mu
    var = jnp.mean(d * d, axis=-1, keepdims=True)
    return d * lax.rsqrt(var + LN_EPS) * w + b


def _inproj_kernel(x_ref, w_ref, cos_ref, sa_ref, sb_ref, oh_ref, q_ref, kvc_ref, ks_ref, kw_ref, vst_ref, vwt_ref,
                   gqkv_ref, gz_ref, conf_ref, small_ref):
    xb = x_ref[...].astype(BF16)
    cos = cos_ref[...]
    sa = sa_ref[...]
    sb = sb_ref[...]
    tm = xb.shape[0]

    def rope(h):
        return h * cos + pltpu.roll(h, LANES - ROPE_DIM // 2, 1) * sa + pltpu.roll(h, ROPE_DIM // 2, 1) * sb

    hq = _dot(xb, w_ref[:, Q_OFF:KV_OFF])
    scale = HEAD_DIM ** -0.5
    for j in range(NSA_WIDTH // LANES):
        q_ref[:, j * LANES:(j + 1) * LANES] = (rope(hq[:, j * LANES:(j + 1) * LANES]) * scale).astype(BF16)
    hkv = _dot(xb, w_ref[:, KV_OFF:GQKV_OFF])
    part = lambda j: hkv[:, j * LANES:(j + 1) * LANES]
    kvc_ref[:, 0:LANES] = rope(part(0)).astype(BF16)
    kvc_ref[:, LANES:2 * LANES] = part(1).astype(BF16)
    ksel = rope(part(2))
    first_half = lax.broadcasted_iota(I32, (tm, LANES), 1) < HEAD_DIM
    oh = oh_ref[...]
    ks_ref[:, 0:LANES] = jnp.where(first_half, ksel, oh).astype(BF16)
    ks_ref[:, LANES:2 * LANES] = jnp.where(first_half, pltpu.roll(ksel, HEAD_DIM, 1), oh).astype(BF16)
    kw_ref[...] = rope(part(4)).astype(BF16)
    vst_ref[0] = part(3).T.astype(BF16)
    vwt = part(5).T.astype(BF16)
    for c in range(tm // Q_BLOCK):
        vwt_ref[c] = vwt[:, c * Q_BLOCK:(c + 1) * Q_BLOCK]
    gqkv_ref[...] = _dot(xb, w_ref[:, GQKV_OFF:GZ_OFF])
    gz_ref[...] = _dot(xb, w_ref[:, GZ_OFF:CONF_OFF])
    conf_ref[...] = _dot(xb, w_ref[:, CONF_OFF:SMALL_OFF])
    small_ref[...] = _dot(xb, w_ref[:, SMALL_OFF:IN_COLS])


def _inproj(x2, w, cos, sa, sb, oh, seq):
    t = x2.shape[0]
    tm = ROW_TILE
    n_pos = seq // tm
    row = lambda i: (i, 0)
    pos = lambda i: (i % n_pos, 0)
    widths = (NSA_WIDTH, 2 * LANES, 2 * LANES, LANES, 3 * GDN_WIDTH, GDN_WIDTH, 2 * CONF_CH, LANES)
    dtypes = (BF16, BF16, BF16, BF16, F32, F32, F32, F32)
    flat_specs = [pl.BlockSpec((tm, wd), row) for wd in widths]
    flat_shapes = [jax.ShapeDtypeStruct((t, wd), dt) for wd, dt in zip(widths, dtypes)]
    n_q = tm // Q_BLOCK
    out_specs = flat_specs[:4] + [pl.BlockSpec((1, LANES, tm), lambda i: (i, 0, 0)),
                                  pl.BlockSpec((n_q, LANES, Q_BLOCK), lambda i: (i, 0, 0))] + flat_specs[4:]
    out_shape = flat_shapes[:4] + [jax.ShapeDtypeStruct((t // tm, LANES, tm), BF16),
                                   jax.ShapeDtypeStruct((t // Q_BLOCK, LANES, Q_BLOCK), BF16)] + flat_shapes[4:]
    return pl.pallas_call(
        _inproj_kernel,
        grid=(t // tm,),
        in_specs=[pl.BlockSpec((tm, D_MODEL), row),
                  pl.BlockSpec((D_MODEL, IN_COLS), lambda i: (0, 0)),
                  pl.BlockSpec((tm, LANES), pos), pl.BlockSpec((tm, LANES), pos), pl.BlockSpec((tm, LANES), pos),
                  pl.BlockSpec((tm, LANES), pos)],
        out_specs=out_specs,
        out_shape=out_shape,
        compiler_params=_params("parallel"),
        name="inproj",
    )(x2, w, cos, sa, sb, oh)


def _block_onehot(seq):
    blk = jnp.arange(seq)[:, None] // SEL_BLOCK
    lane = jnp.arange(LANES)[None, :]
    return jnp.where(lane - HEAD_DIM == blk, 1.0, 0.0).astype(F32)


def _rope_tables(seq):
    half = ROPE_DIM // 2
    inv = ROPE_THETA ** (-jnp.arange(0, ROPE_DIM, 2, dtype=F32) / ROPE_DIM)
    ang = jnp.arange(seq, dtype=F32)[:, None] * inv[None, :]
    c, s = jnp.cos(ang), jnp.sin(ang)
    ones = jnp.ones((seq, HEAD_DIM - ROPE_DIM), F32)
    zeros = jnp.zeros((seq, HEAD_DIM - ROPE_DIM), F32)
    zh = jnp.zeros((seq, half), F32)
    cos_h = jnp.concatenate([c, c, ones], axis=1)
    sa_h = jnp.concatenate([-s, zh, zeros], axis=1)
    sb_h = jnp.concatenate([zh, s, zeros], axis=1)
    rep = LANES // HEAD_DIM
    return jnp.tile(cos_h, (1, rep)), jnp.tile(sa_h, (1, rep)), jnp.tile(sb_h, (1, rep))


def _permute_w_in(w):
    o_q, o_kv = 0, NSA_WIDTH
    o_gate = o_kv + 6 * KV_WIDTH
    o_gqkv = o_gate + 3 * NSA_HEADS
    o_gz = o_gqkv + 3 * GDN_WIDTH
    o_b = o_gz + GDN_WIDTH
    o_a = o_b + GDN_HEADS
    o_conf = o_a + GDN_HEADS
    pad = jnp.zeros((w.shape[0], LANES - N_SMALL), w.dtype)
    return jnp.concatenate([w[:, o_q:o_gate], w[:, o_gqkv:o_b], w[:, o_conf:], w[:, o_gate:o_gqkv],
                            w[:, o_b:o_conf], pad], axis=1).astype(BF16)


def _compress_kernel(ch_ref, pet_ref, peb_ref, w1t_ref, w1b_ref, w2_ref, o_ref, ot_ref):
    ch = ch_ref[0, 0].astype(F32)
    top = _dot((ch + pet_ref[0]).astype(BF16), w1t_ref[0])
    bot = _dot((ch + peb_ref[0]).astype(BF16), w1b_ref[0])
    n = bot.shape[0]
    hid = top + pltpu.roll(bot, n - 1, 0)
    out = _dot(_silu(hid).astype(BF16), w2_ref[0])
    o_ref[0, 0] = out.astype(o_ref.dtype)
    ot_ref[0, 0] = jnp.concatenate([out, jnp.zeros_like(out)], axis=1).T[:HEAD_DIM].astype(ot_ref.dtype)


def _compress(chunks, pe_top, pe_bot, w1_top, w1_bot, w2):
    _, bh, n_chunk, feat = chunks.shape
    per = lambda r, i: (r, 0, 0)
    return pl.pallas_call(
        _compress_kernel,
        grid=(2, bh),
        in_specs=[pl.BlockSpec((1, 1, n_chunk, feat), lambda r, i: (r, i, 0, 0)),
                  pl.BlockSpec((1, 1, feat), per), pl.BlockSpec((1, 1, feat), per),
                  pl.BlockSpec((1, feat, CMP_HIDDEN), per), pl.BlockSpec((1, feat, CMP_HIDDEN), per),
                  pl.BlockSpec((1, CMP_HIDDEN, HEAD_DIM), per)],
        out_specs=[pl.BlockSpec((1, 1, n_chunk, HEAD_DIM), lambda r, i: (r, i, 0, 0)),
                   pl.BlockSpec((1, 1, HEAD_DIM, n_chunk), lambda r, i: (r, i, 0, 0))],
        out_shape=[jax.ShapeDtypeStruct((2, bh, n_chunk, HEAD_DIM), BF16),
                   jax.ShapeDtypeStruct((2, bh, HEAD_DIM, n_chunk), BF16)],
        compiler_params=_params("parallel", "parallel"),
        name="nsa_compress",
    )(chunks, pe_top, pe_bot, w1_top, w1_bot, w2)


def _nsa_kernel(q_ref, kc_ref, vct_ref, ks_ref, vst_ref, kw_ref, vwt_ref, gl_ref, ovt_ref, o_ref, *, seq, n_sel):
    g, qn = NSA_GROUP, Q_BLOCK
    lanes = g * qn
    n_c = seq // CMP_STRIDE
    head = pl.program_id(1)
    qb = pl.program_id(2)
    s0 = qb * qn
    tile = lambda a: jnp.concatenate([a] * g, axis=1)
    qt4 = q_ref[...].astype(F32).T
    q_t = jnp.concatenate([qt4[i * HEAD_DIM:(i + 1) * HEAD_DIM] for i in range(g)], axis=1).astype(BF16)
    t_row = s0 + (lax.broadcasted_iota(I32, (1, lanes), 1) & (qn - 1))
    ones_rows = jnp.ones((16, ROW_TILE), BF16)

    sc = _dot(kc_ref[0, 0], q_t)
    cend = lax.broadcasted_iota(I32, (n_c, lanes), 0) * CMP_STRIDE + (CMP_LEN - 1)
    sc = jnp.where(cend <= t_row, sc, -jnp.inf)
    m = jnp.max(sc, axis=0, keepdims=True)
    m = jnp.where(m == -jnp.inf, 0.0, m)
    e = jnp.exp(sc - m)
    pc = e * (1.0 / jnp.maximum(jnp.sum(e, axis=0, keepdims=True), 1e-30))
    o_cmp = _dot(vct_ref[0, 0], pc.astype(BF16))

    pcs = pc[:, 0:qn]
    for i in range(1, g):
        pcs = pcs + pc[:, i * qn:(i + 1) * qn]
    imp = jnp.dot(ovt_ref[...], pcs, preferred_element_type=F32, precision=lax.Precision.HIGHEST)
    n_blk = imp.shape[0]
    blk = lax.broadcasted_iota(I32, (n_blk, qn), 0)
    cur = (s0 + lax.broadcasted_iota(I32, (n_blk, qn), 1)) // SEL_BLOCK
    forced = (blk == 0) | (blk == cur) | (blk == cur - 1)
    causal = blk <= cur
    imp = jnp.where(causal, imp + jnp.where(forced, FORCE_BONUS, 0.0), -jnp.inf)
    rank = jnp.zeros((n_blk, qn), F32)
    for i in range(n_blk):
        vi = imp[i:i + 1, :]
        ahead = (vi > imp) | ((vi == imp) & (blk > i))
        rank = rank + jnp.where(ahead, 1.0, 0.0)
    sel_bias = jnp.where((rank < n_sel) & causal, 0.0, NEG).astype(BF16)
    q_aug = jnp.concatenate([q_t, tile(sel_bias)], axis=0)

    zeros = jnp.zeros_like(q_t)
    q_win = jnp.where(head == 0, jnp.concatenate([q_t, zeros], axis=0), jnp.concatenate([zeros, q_t], axis=0))
    n_wb = WINDOW // qn + 1
    sw = []
    for kb in range(n_wb):
        bi = qb - (n_wb - 1) + kb
        k0 = pl.multiple_of(jnp.maximum(bi, 0) * qn, qn)
        pos = bi * qn + lax.broadcasted_iota(I32, (qn, qn), 0)
        tq = s0 + lax.broadcasted_iota(I32, (qn, qn), 1)
        ok = (pos >= 0) & (pos <= tq) & (pos > tq - WINDOW)
        sw.append(_dot(kw_ref[pl.ds(k0, qn), :], q_win) + tile(jnp.where(ok, 0.0, NEG)))
    m_w = jnp.max(sw[0], axis=0, keepdims=True)
    for kb in range(1, n_wb):
        m_w = jnp.maximum(m_w, jnp.max(sw[kb], axis=0, keepdims=True))
    acc_w = jnp.zeros((HEAD_DIM + 16, lanes), F32)
    for kb in range(n_wb):
        vt = jnp.concatenate([vwt_ref[0, jnp.maximum(qb - (n_wb - 1) + kb, 0)], ones_rows[:, 0:qn]], axis=0)
        acc_w = acc_w + _dot(vt, jnp.exp(sw[kb] - m_w).astype(BF16))
    o_win = acc_w[0:HEAD_DIM] * (1.0 / acc_w[HEAD_DIM:HEAD_DIM + 1])

    kstep = ROW_TILE
    c_last = s0 // kstep

    def sel_scores(c):
        k0 = pl.multiple_of(c * kstep, kstep)
        return _dot(ks_ref[pl.ds(k0, kstep), :], q_aug)

    def sel_values(c):
        return jnp.concatenate([vst_ref[0, c], ones_rows], axis=0)

    def sel_update(c, m_i, acc, s):
        m_new = jnp.maximum(m_i, jnp.max(s, axis=0, keepdims=True))
        p = jnp.exp(s - m_new).astype(BF16)
        return m_new, jnp.exp(m_i - m_new) * acc + _dot(sel_values(c), p)

    def sel_step(c, carry):
        m_i, acc, s = carry
        s_next = sel_scores(c + 1)
        return sel_update(c, m_i, acc, s) + (s_next,)

    init = (jnp.full((1, lanes), NEG, F32), jnp.zeros((HEAD_DIM + 16, lanes), F32), sel_scores(0))
    m_s, acc_s, s_last = lax.fori_loop(0, c_last, sel_step, init)
    key = c_last * kstep + lax.broadcasted_iota(I32, (kstep, qn), 0)
    future = jnp.where(key <= s0 + lax.broadcasted_iota(I32, (kstep, qn), 1), 0.0, NEG)
    _, acc_s = sel_update(c_last, m_s, acc_s, s_last + tile(future))
    o_sel = acc_s[0:HEAD_DIM] * (1.0 / acc_s[HEAD_DIM:HEAD_DIM + 1])

    gate = _sigmoid(gl_ref[0, 0, 0])
    o = o_cmp * gate[0:1] + o_sel * gate[1:2] + o_win * gate[2:3]
    o4 = jnp.concatenate([o[:, i * qn:(i + 1) * qn] for i in range(g)], axis=0)
    o_ref[...] = o4.T.astype(o_ref.dtype)


def _overlap_t(n_c, n_sb):
    c0 = jnp.arange(n_c)[None, :] * CMP_STRIDE
    b0 = jnp.arange(HEAD_DIM)[:, None] * SEL_BLOCK
    ov = jnp.minimum(c0 + CMP_LEN, b0 + SEL_BLOCK) - jnp.maximum(c0, b0)
    ov = (jnp.maximum(ov, 0) / CMP_STRIDE).astype(F32)
    return jnp.where(jnp.arange(HEAD_DIM)[:, None] < n_sb, ov, 0.0)


def _nsa_attention(q, kc, vct, ks, vst, kw, vwt, gl, ov_t, b, seq):
    n_c = seq // CMP_STRIDE
    n_sb = seq // SEL_BLOCK
    n_qb = seq // Q_BLOCK
    n_sel = min(N_SEL, n_sb)
    t = b * seq
    width = NSA_GROUP * HEAD_DIM
    vst4 = vst.reshape(b, seq // ROW_TILE, LANES, ROW_TILE)
    vwt4 = vwt.reshape(b, n_qb, LANES, Q_BLOCK)
    return pl.pallas_call(
        functools.partial(_nsa_kernel, seq=seq, n_sel=n_sel),
        grid=(b, NSA_KV_HEADS, n_qb),
        in_specs=[pl.BlockSpec((Q_BLOCK, width), lambda bi, h, i: (bi * n_qb + i, h)),
                  pl.BlockSpec((1, 1, n_c, HEAD_DIM), lambda bi, h, i: (0, bi * NSA_KV_HEADS + h, 0, 0)),
                  pl.BlockSpec((1, 1, HEAD_DIM, n_c), lambda bi, h, i: (1, bi * NSA_KV_HEADS + h, 0, 0)),
                  pl.BlockSpec((seq, LANES), lambda bi, h, i: (bi, h)),
                  pl.BlockSpec((1, seq // ROW_TILE, HEAD_DIM, ROW_TILE), lambda bi, h, i: (bi, 0, h, 0)),
                  pl.BlockSpec((seq, LANES), lambda bi, h, i: (bi, 0)),
                  pl.BlockSpec((1, n_qb, HEAD_DIM, Q_BLOCK), lambda bi, h, i: (bi, 0, h, 0)),
                  pl.BlockSpec((1, 1, 1, 3, NSA_GROUP * Q_BLOCK), lambda bi, h, i: (bi, h, i, 0, 0)),
                  pl.BlockSpec((HEAD_DIM, n_c), lambda bi, h, i: (0, 0))],
        out_specs=pl.BlockSpec((Q_BLOCK, width), lambda bi, h, i: (bi * n_qb + i, h)),
        out_shape=jax.ShapeDtypeStruct((t, NSA_WIDTH), BF16),
        compiler_params=_params("parallel", "parallel", "arbitrary"),
        name="nsa_attention",
    )(q, kc, vct, ks, vst4, kw, vwt4, gl, ov_t)


def _gdn_conv_kernel(cur_ref, prev_ref, w_ref, o_ref, ext_ref):
    ts = cur_ref.shape[1]
    halo = prev_ref.shape[1]
    first = pl.program_id(1) == 0
    ext_ref[0:halo, :] = jnp.where(first, 0.0, prev_ref[0])
    ext_ref[halo:halo + ts, :] = cur_ref[0]
    acc = cur_ref[0] * w_ref[GDN_CONV - 1:GDN_CONV, :]
    for k in range(GDN_CONV - 1):
        acc = acc + ext_ref[pl.ds(halo - (GDN_CONV - 1) + k, ts), :] * w_ref[k:k + 1, :]
    o_ref[0] = _silu(acc)


def _gdn_conv(x3, w):
    b, seq, ch = x3.shape
    ts = min(ROW_TILE, seq)
    halo = 8
    per_tile = ts // halo
    return pl.pallas_call(
        _gdn_conv_kernel,
        grid=(b, seq // ts),
        in_specs=[pl.BlockSpec((1, ts, ch), lambda bi, i: (bi, i, 0)),
                  pl.BlockSpec((1, halo, ch), lambda bi, i: (bi, jnp.maximum(i * per_tile - 1, 0), 0)),
                  pl.BlockSpec((GDN_CONV, ch), lambda bi, i: (0, 0))],
        out_specs=pl.BlockSpec((1, ts, ch), lambda bi, i: (bi, i, 0)),
        out_shape=jax.ShapeDtypeStruct((b, seq, ch), F32),
        scratch_shapes=[pltpu.VMEM((ts + halo, ch), F32)],
        compiler_params=_params("parallel", "parallel"),
        name="gdn_conv",
    )(x3, x3, w)


def _l2norm(v):
    return v * lax.rsqrt(jnp.sum(v * v, axis=-1, keepdims=True) + 1e-6)


def _softplus(v):
    return jnp.maximum(v, 0.0) + jnp.log1p(jnp.exp(-jnp.abs(v)))


def _gdn_kernel(q_ref, k_ref, v_ref, z_ref, b_ref, ac_ref, ar_ref, alog_ref, dtb_ref, nw_ref, o_ref, state_ref,
                *, chunks):
    c_len = GDN_CHUNK

    @pl.when(pl.program_id(1) == 0)
    def _():
        state_ref[...] = jnp.zeros_like(state_ref)

    ri = lax.broadcasted_iota(I32, (c_len, c_len), 0)
    ci = lax.broadcasted_iota(I32, (c_len, c_len), 1)
    lower = (ri >= ci)[None]
    strict = (ri > ci)[None]
    upper = (ri <= ci)[None]
    eye = jnp.where(ri == ci, 1.0, 0.0)[None]
    nb, nh = z_ref.shape[0], z_ref.shape[1]
    n = nb * nh
    rate = jnp.concatenate([-jnp.exp(alog_ref[...])] * nb, axis=0)
    dtb = jnp.concatenate([dtb_ref[...]] * nb, axis=0)
    nw = nw_ref[...]

    def chunk(c, carry):
        r0 = pl.multiple_of(c * c_len, c_len)
        rows = pl.ds(r0, c_len)
        q = _l2norm(q_ref[0, :, :, rows, :].reshape(n, c_len, HEAD_DIM)) * HEAD_DIM ** -0.5
        k = _l2norm(k_ref[0, :, :, rows, :].reshape(n, c_len, HEAD_DIM))
        v = v_ref[0, :, :, rows, :].reshape(n, c_len, HEAD_DIM)
        beta = _sigmoid(b_ref[:, :, rows, :].reshape(n, c_len, 1))
        g_col = rate * _softplus(ac_ref[:, :, rows, :].reshape(n, c_len, 1) + dtb)
        g_row = rate * _softplus(ar_ref[:, :, c].reshape(n, 1, c_len) + dtb)
        gc_col = jnp.sum(jnp.where(lower, g_row, 0.0), axis=2, keepdims=True)
        gc_row = jnp.sum(jnp.where(upper, g_col, 0.0), axis=1, keepdims=True)
        decay = jnp.exp(jnp.where(lower, gc_col - gc_row, -jnp.inf))
        kb = k * beta
        a = jnp.where(strict, _bmm_nt(kb, k) * decay, 0.0)
        inv = eye - a
        pw = a
        for _ in range(int(np.log2(c_len)) - 1):
            pw = _bmm(pw, pw)
            inv = inv + _bmm(inv, pw)
        e_col = jnp.exp(gc_col)
        u = _bmm(inv, v * beta)
        w = _bmm(inv, kb * e_col)
        qk = _bmm_nt(q, k) * decay
        g_last = gc_col[:, c_len - 1:c_len, :]
        k_dec = k * jnp.exp(g_last - gc_col)
        state = state_ref[...]
        v_new = u - _bmm(w, state)
        o = _bmm(q * e_col, state) + _bmm(qk, v_new)
        state_ref[...] = state * jnp.exp(g_last) + _bmm(jnp.swapaxes(k_dec, 1, 2), v_new)
        o = o * lax.rsqrt(jnp.mean(o * o, axis=-1, keepdims=True) + 1e-6) * nw
        o = o * _silu(z_ref[:, :, rows, :].reshape(n, c_len, HEAD_DIM))
        o_ref[:, :, rows, :] = o.reshape(nb, nh, c_len, HEAD_DIM).astype(o_ref.dtype)
        return carry

    lax.fori_loop(0, chunks, chunk, 0)


def _gdn(qkvh, zh, b_col, a_col, a_row, a_log, dt_bias, norm_w):
    _, b, h, seq, dh = qkvh.shape
    chunks = GDN_BLOCK_CHUNKS
    rows = chunks * GDN_CHUNK
    nb = GDN_BLOCK_BATCH if b % GDN_BLOCK_BATCH == 0 else 1

    def qkv_spec(r):
        return pl.BlockSpec((1, nb, h, rows, dh), lambda bi, i: (r, bi, 0, i, 0))

    col = pl.BlockSpec((nb, h, rows, 1), lambda bi, i: (bi, 0, i, 0))
    head = pl.BlockSpec((h, 1, 1), lambda bi, i: (0, 0, 0))
    return pl.pallas_call(
        functools.partial(_gdn_kernel, chunks=chunks),
        grid=(b // nb, seq // rows),
        in_specs=[qkv_spec(0), qkv_spec(1), qkv_spec(2),
                  pl.BlockSpec((nb, h, rows, dh), lambda bi, i: (bi, 0, i, 0)),
                  col, col,
                  pl.BlockSpec((nb, h, chunks, 1, GDN_CHUNK), lambda bi, i: (bi, 0, i, 0, 0)),
                  head, head,
                  pl.BlockSpec((1, dh), lambda bi, i: (0, 0))],
        out_specs=pl.BlockSpec((nb, h, rows, dh), lambda bi, i: (bi, 0, i, 0)),
        out_shape=jax.ShapeDtypeStruct((b, h, seq, dh), BF16),
        scratch_shapes=[pltpu.VMEM((nb * h, dh, dh), F32)],
        compiler_params=_params("parallel", "arbitrary"),
        name="gdn_delta_rule",
    )(qkvh, qkvh, qkvh, zh, b_col, a_col, a_row, a_log, dt_bias, norm_w)


def _conf_kernel(cur_ref, prev_ref, w_ref, b_ref, lnw_ref, lnb_ref, o_ref, ext_ref):
    ts = cur_ref.shape[1]
    halo = prev_ref.shape[1]
    first = pl.program_id(1) == 0

    def glu(blk):
        return blk[:, :CONF_CH] * _sigmoid(blk[:, CONF_CH:])

    ext_ref[0:halo, :] = jnp.where(first, 0.0, glu(prev_ref[0]))
    ext_ref[halo:halo + ts, :] = glu(cur_ref[0])
    acc = jnp.zeros((ts, CONF_CH), F32) + b_ref[...]
    for k in range(CONF_WIDTH):
        acc = acc + ext_ref[pl.ds(halo - (CONF_WIDTH - 1) + k, ts), :] * w_ref[k:k + 1, :]
    o_ref[0] = _silu(_layer_norm(acc, lnw_ref[...], lnb_ref[...])).astype(o_ref.dtype)


def _conformer(x3, w, bias, ln_w, ln_b):
    b, seq, ch = x3.shape
    ts = min(ROW_TILE, seq)
    halo = 32
    per_tile = ts // halo
    vec = pl.BlockSpec((1, CONF_CH), lambda bi, i: (0, 0))
    return pl.pallas_call(
        _conf_kernel,
        grid=(b, seq // ts),
        in_specs=[pl.BlockSpec((1, ts, ch), lambda bi, i: (bi, i, 0)),
                  pl.BlockSpec((1, halo, ch), lambda bi, i: (bi, jnp.maximum(i * per_tile - 1, 0), 0)),
                  pl.BlockSpec((CONF_WIDTH, CONF_CH), lambda bi, i: (0, 0)), vec, vec, vec],
        out_specs=pl.BlockSpec((1, ts, CONF_CH), lambda bi, i: (bi, i, 0)),
        out_shape=jax.ShapeDtypeStruct((b, seq, CONF_CH), BF16),
        scratch_shapes=[pltpu.VMEM((ts + halo, CONF_CH), F32)],
        compiler_params=_params("parallel", "parallel"),
        name="conformer_conv",
    )(x3, x3, w, bias, ln_w, ln_b)


def _outproj_kernel(yn_ref, yg_ref, yc_ref, x_ref, w_ref, lnw_ref, lnb_ref, wr_ref, br_ref, x1_ref, x1b_ref,
                    route_ref, counts_ref, carry_ref):
    mix = _dot(yn_ref[...], w_ref[0:NSA_WIDTH, :])
    mix = mix + _dot(yg_ref[...], w_ref[NSA_WIDTH:NSA_WIDTH + GDN_WIDTH, :])
    mix = mix + _dot(yc_ref[...], w_ref[NSA_WIDTH + GDN_WIDTH:, :])
    x1 = _layer_norm(DEEPNORM_ALPHA * x_ref[...] + mix, lnw_ref[...], lnb_ref[...])
    x1_ref[...] = x1
    xb = x1.astype(BF16)
    x1b_ref[...] = xb
    _route_rows(_dot(xb, wr_ref[...]) + br_ref[...], route_ref, counts_ref, carry_ref)


def _outproj(y_nsa, y_gdn, y_conf, x2, w, ln_w, ln_b, wr, br):
    t = x2.shape[0]
    tm = min(ROW_TILE, t)
    row = lambda i: (i, 0)
    fixed = lambda i: (0, 0)
    x1, x1b, route, counts = pl.pallas_call(
        _outproj_kernel,
        grid=(t // tm,),
        in_specs=[pl.BlockSpec((tm, NSA_WIDTH), row), pl.BlockSpec((tm, GDN_WIDTH), row),
                  pl.BlockSpec((tm, CONF_CH), row), pl.BlockSpec((tm, D_MODEL), row),
                  pl.BlockSpec((D_MODEL, D_MODEL), fixed), pl.BlockSpec((1, D_MODEL), fixed),
                  pl.BlockSpec((1, D_MODEL), fixed), pl.BlockSpec((D_MODEL, LANES), fixed),
                  pl.BlockSpec((1, LANES), fixed)],
        out_specs=[pl.BlockSpec((tm, D_MODEL), row), pl.BlockSpec((tm, D_MODEL), row), pl.BlockSpec((tm, LANES), row),
                   pl.BlockSpec((1, LANES), fixed)],
        out_shape=[jax.ShapeDtypeStruct((t, D_MODEL), F32), jax.ShapeDtypeStruct((t, D_MODEL), BF16),
                   jax.ShapeDtypeStruct((t, LANES), F32), jax.ShapeDtypeStruct((1, LANES), F32)],
        scratch_shapes=[pltpu.VMEM((1, LANES), F32)],
        compiler_params=_params("arbitrary"),
        name="outproj_ln_route",
    )(y_nsa, y_gdn, y_conf, x2, w, ln_w, ln_b, wr, br)
    return x1, x1b, (route, counts)


def _route_rows(lg, route_ref, counts_ref, carry_ref):
    tm = lg.shape[0]

    @pl.when(pl.program_id(0) == 0)
    def _():
        carry_ref[...] = jnp.zeros_like(carry_ref)

    lane = lax.broadcasted_iota(I32, (tm, LANES), 1)
    lane_f = lane.astype(F32)

    def top1(vals):
        best = jnp.max(vals, axis=1, keepdims=True)
        idx = jnp.min(jnp.where(vals == best, lane_f, float(LANES)), axis=1, keepdims=True)
        return best, idx

    gl = jnp.where(lane < N_GROUPS, lg, -jnp.inf)
    g_best, g_idx = top1(gl)
    g_w = 1.0 / jnp.sum(jnp.exp(gl - g_best), axis=1, keepdims=True)
    lo = N_GROUPS + g_idx * EXPERTS_PER_GROUP
    el = jnp.where((lane_f >= lo) & (lane_f < lo + EXPERTS_PER_GROUP), lg, -jnp.inf)
    e1, i1 = top1(el)
    e2, i2 = top1(jnp.where(lane_f == i1, -jnp.inf, el))
    r = jnp.exp(e2 - e1)
    w1 = g_w / (1.0 + r)
    w2 = g_w * r / (1.0 + r)
    hit1 = lane_f == i1
    hit2 = lane_f == i2
    onehot = jnp.where(hit1 | hit2, 1.0, 0.0)
    ri = lax.broadcasted_iota(I32, (tm, tm), 0)
    ci = lax.broadcasted_iota(I32, (tm, tm), 1)
    before = jnp.where(ri > ci, 1.0, 0.0).astype(BF16)
    seen = carry_ref[...] + _dot(before, onehot.astype(BF16))
    p1 = jnp.sum(jnp.where(hit1, seen, 0.0), axis=1, keepdims=True)
    p2 = jnp.sum(jnp.where(hit2, seen, 0.0), axis=1, keepdims=True)
    total = carry_ref[...] + jnp.sum(onehot, axis=0, keepdims=True)
    carry_ref[...] = total
    counts_ref[...] = total
    cols = (i1 - N_GROUPS, i2 - N_GROUPS, w1, w2, p1, p2)
    out = jnp.zeros((tm, LANES), F32)
    for j, cval in enumerate(cols):
        out = jnp.where(lane == j, cval, out)
    route_ref[...] = out


def _expert_kernel(be_ref, nu_ref, x_ref, wg_ref, wu_ref, wd_ref, o_ref, wgb_ref, wub_ref, wdb_ref):
    i = pl.program_id(0)
    changed = (i == 0) | (be_ref[i] != be_ref[jnp.maximum(i - 1, 0)])

    @pl.when(changed)
    def _():
        wgb_ref[...] = wg_ref[0, 0].astype(BF16)
        wub_ref[...] = wu_ref[0, 0].astype(BF16)
        wdb_ref[...] = wd_ref[0, 0].astype(BF16)

    @pl.when(i < nu_ref[0])
    def _():
        xb = x_ref[...]
        hid = _silu(_dot(xb, wgb_ref[...])) * _dot(xb, wub_ref[...])
        o_ref[...] = _dot(hid.astype(BF16), wdb_ref[...])

    @pl.when(i >= nu_ref[0])
    def _():
        o_ref[...] = jnp.zeros_like(o_ref)


def _experts(blk_expert, n_used, buf, w_gate, w_up, w_down, layer):
    rows = buf.shape[0]
    n_blocks = rows // MOE_ROWS
    wspec_in = pl.BlockSpec((1, 1, D_MODEL, EXPERT_FF), lambda i, be, nu: (layer, be[i], 0, 0))
    wspec_out = pl.BlockSpec((1, 1, EXPERT_FF, D_MODEL), lambda i, be, nu: (layer, be[i], 0, 0))
    return pl.pallas_call(
        _expert_kernel,
        grid_spec=pltpu.PrefetchScalarGridSpec(
            num_scalar_prefetch=2,
            grid=(n_blocks,),
            in_specs=[pl.BlockSpec((MOE_ROWS, D_MODEL), lambda i, be, nu: (i, 0)), wspec_in, wspec_in, wspec_out],
            out_specs=pl.BlockSpec((MOE_ROWS, D_MODEL), lambda i, be, nu: (i, 0)),
            scratch_shapes=[pltpu.VMEM((D_MODEL, EXPERT_FF), BF16), pltpu.VMEM((D_MODEL, EXPERT_FF), BF16),
                            pltpu.VMEM((EXPERT_FF, D_MODEL), BF16)]),
        out_shape=jax.ShapeDtypeStruct((rows, D_MODEL), F32),
        compiler_params=_params("arbitrary"),
        name="moe_experts",
    )(blk_expert, n_used, buf, w_gate, w_up, w_down)


def _combine_kernel(x1_ref, y0_ref, y1_ref, route_ref, lnw_ref, lnb_ref, o_ref):
    route = route_ref[...]
    moe = y0_ref[...] * route[:, 2:3] + y1_ref[...] * route[:, 3:4]
    o_ref[...] = _layer_norm(DEEPNORM_ALPHA * x1_ref[...] + moe, lnw_ref[...], lnb_ref[...])


def _combine(x1, y0, y1, route, ln_w, ln_b):
    t = x1.shape[0]
    tm = min(ROW_TILE, t)
    row = lambda i: (i, 0)
    fixed = lambda i: (0, 0)
    big = pl.BlockSpec((tm, D_MODEL), row)
    return pl.pallas_call(
        _combine_kernel,
        grid=(t // tm,),
        in_specs=[big, big, big, pl.BlockSpec((tm, LANES), row),
                  pl.BlockSpec((1, D_MODEL), fixed), pl.BlockSpec((1, D_MODEL), fixed)],
        out_specs=big,
        out_shape=jax.ShapeDtypeStruct((t, D_MODEL), F32),
        compiler_params=_params("parallel"),
        name="moe_combine_ln",
    )(x1, y0, y1, route, ln_w, ln_b)


def _mix_heads(x2, l, b, seq, tables, w_in, nsa_cmp_pe, nsa_cmp_w1, nsa_cmp_w2, gdn_conv_w, gdn_a_log, gdn_dt_bias,
               gdn_norm_w, conf_dw_w, conf_dw_b, conf_ln_w, conf_ln_b):
    cos, sa, sb, oh, ov_t = tables
    t = b * seq
    n_c = seq // CMP_STRIDE
    n_gc = seq // GDN_CHUNK
    n_qb = seq // Q_BLOCK
    half = CMP_LEN // 2
    feat = CMP_STRIDE * HEAD_DIM
    q, kvc, ks, kw, vst, vwt, gqkv, gz, conf_in, small = _inproj(x2, _permute_w_in(w_in[l]), cos, sa, sb, oh, seq)

    chunks = kvc.reshape(b, seq, 2, NSA_KV_HEADS, HEAD_DIM).transpose(2, 0, 3, 1, 4)
    chunks = chunks.reshape(2, b * NSA_KV_HEADS, n_c, feat)
    pe = nsa_cmp_pe[l]
    kc, vct = _compress(chunks, pe[:, :half].reshape(2, 1, feat), pe[:, half:].reshape(2, 1, feat),
                        nsa_cmp_w1[l][:, :feat].astype(BF16), nsa_cmp_w1[l][:, feat:].astype(BF16),
                        nsa_cmp_w2[l].astype(BF16))
    gl = small[:, :3 * NSA_HEADS].reshape(b, n_qb, Q_BLOCK, NSA_KV_HEADS, NSA_GROUP, 3)
    gl = gl.transpose(0, 3, 1, 5, 4, 2).reshape(b, NSA_KV_HEADS, n_qb, 3, NSA_GROUP * Q_BLOCK)
    y_nsa = _nsa_attention(q, kc, vct, ks, vst, kw, vwt, gl, ov_t, b, seq)

    conv = _gdn_conv(gqkv.reshape(b, seq, 3 * GDN_WIDTH), gdn_conv_w[l])
    qkvh = conv.reshape(b, seq, 3, GDN_HEADS, HEAD_DIM).transpose(2, 0, 3, 1, 4)
    zh = gz.reshape(b, seq, GDN_HEADS, HEAD_DIM).transpose(0, 2, 1, 3)
    o_b = 3 * NSA_HEADS
    b_t = small[:, o_b:o_b + GDN_HEADS].reshape(b, seq, GDN_HEADS).transpose(0, 2, 1)
    a_t = small[:, o_b + GDN_HEADS:o_b + 2 * GDN_HEADS].reshape(b, seq, GDN_HEADS).transpose(0, 2, 1)
    y_gdn = _gdn(qkvh, zh, b_t[..., None], a_t[..., None], a_t.reshape(b, GDN_HEADS, n_gc, 1, GDN_CHUNK),
                 gdn_a_log[l].reshape(GDN_HEADS, 1, 1), gdn_dt_bias[l].reshape(GDN_HEADS, 1, 1),
                 gdn_norm_w[l].reshape(1, HEAD_DIM))
    y_gdn = y_gdn.transpose(0, 2, 1, 3).reshape(t, GDN_WIDTH)

    y_conf = _conformer(conf_in.reshape(b, seq, 2 * CONF_CH), conf_dw_w[l], conf_dw_b[l].reshape(1, CONF_CH),
                        conf_ln_w[l].reshape(1, CONF_CH), conf_ln_b[l].reshape(1, CONF_CH)).reshape(t, CONF_CH)
    return y_nsa, y_gdn, y_conf


def _moe_block(x1, x1b, routing, l, ln_w, ln_b, moe_w_gate, moe_w_up, moe_w_down):
    t = x1.shape[0]
    tk = 2 * t
    n_blocks = tk // MOE_ROWS + N_EXPERTS
    route, counts = routing
    eid = route[:, 0:2].astype(I32)
    pos = route[:, 4:6].astype(I32)
    cnt = counts[0, N_GROUPS:N_GROUPS + N_EXPERTS].astype(I32)
    padded = (cnt + MOE_ROWS - 1) // MOE_ROWS * MOE_ROWS
    pad_end = jnp.cumsum(padded)
    pad_start = pad_end - padded
    first_row = jnp.sum(jnp.where(eid[..., None] == jnp.arange(N_EXPERTS, dtype=I32), pad_start, 0), axis=-1)
    dest = first_row + pos
    spare = jnp.arange(n_blocks * MOE_ROWS, dtype=I32) % t
    src = spare.at[dest.reshape(tk)].set(jnp.arange(tk, dtype=I32) // 2)
    blk_row = jnp.arange(n_blocks, dtype=I32)[:, None] * MOE_ROWS
    blk_expert = jnp.minimum(jnp.sum((pad_end[None, :] <= blk_row).astype(I32), axis=1), N_EXPERTS - 1)
    n_used = (pad_end[-1:] // MOE_ROWS).astype(I32)
    y_buf = _experts(blk_expert, n_used, x1b[src], moe_w_gate, moe_w_up, moe_w_down, l)
    return _combine(x1, y_buf[dest[:, 0]], y_buf[dest[:, 1]], route, ln_w, ln_b)


def kernel(x, w_in, w_out, nsa_cmp_pe, nsa_cmp_w1, nsa_cmp_w2, gdn_conv_w, gdn_a_log, gdn_dt_bias, gdn_norm_w, conf_dw_w, conf_dw_b, conf_ln_w, conf_ln_b, ln1_w, ln1_b, ln2_w, ln2_b, moe_w_group, moe_b_group, moe_w_expert, moe_b_expert, moe_w_gate, moe_w_up, moe_w_down):
    b, seq, d = x.shape
    depth = w_in.shape[0]
    assert d == D_MODEL and seq % ROW_TILE == 0 and seq >= WINDOW + Q_BLOCK
    t = b * seq
    assert seq // SEL_BLOCK <= HEAD_DIM
    tables = _rope_tables(seq) + (_block_onehot(seq), _overlap_t(seq // CMP_STRIDE, seq // SEL_BLOCK))
    w_out_b = w_out.astype(BF16)
    n_pad = LANES - N_GROUPS - N_EXPERTS
    x2 = x.reshape(t, d)
    for l in range(depth):
        y_nsa, y_gdn, y_conf = _mix_heads(x2, l, b, seq, tables, w_in, nsa_cmp_pe, nsa_cmp_w1, nsa_cmp_w2, gdn_conv_w,
                                          gdn_a_log, gdn_dt_bias, gdn_norm_w, conf_dw_w, conf_dw_b, conf_ln_w, conf_ln_b)
        wr = jnp.concatenate([moe_w_group[l], moe_w_expert[l], jnp.zeros((d, n_pad), F32)], axis=1).astype(BF16)
        br = jnp.concatenate([moe_b_group[l], moe_b_expert[l], jnp.zeros((n_pad,), F32)]).reshape(1, LANES)
        x1, x1b, routing = _outproj(y_nsa, y_gdn, y_conf, x2, w_out_b[l], ln1_w[l].reshape(1, d),
                                   ln1_b[l].reshape(1, d), wr, br)
        x2 = _moe_block(x1, x1b, routing, l, ln2_w[l].reshape(1, d), ln2_b[l].reshape(1, d),
                        moe_w_gate, moe_w_up, moe_w_down)
    return x2.reshape(b, seq, d)
```
